```python
import jax, jax.numpy as jnp
from jax import lax
import numpy as np

D_MODEL = 1024
BATCH = 16
SEQ = 2048
DEPTH = 4
DEC_BATCH = 8
DEC_SEQ = 4096
PAST_LEN = 128

N_META = 16
GRID_W = 64
CONV_DIM = 256
CONV_WIDTH = 3
NA_HEADS = 4
NA_HEAD_DIM = 64
NA_DIM = NA_HEADS * NA_HEAD_DIM
NA_MAX_WIN_H = 8
NA_WIN_W = 16
MLA_HEADS = 8
MLA_NOPE_DIM = 64
MLA_ROPE_DIM = 32
MLA_V_DIM = 64
MLA_Q_RANK = 768
MLA_KV_RANK = 256
MLA_DIM = MLA_HEADS * MLA_V_DIM
MIX_DIM = CONV_DIM + NA_DIM + MLA_DIM
IN_PROJ_DIM = 3 * CONV_DIM + 3 * NA_DIM + MLA_Q_RANK + MLA_KV_RANK + MLA_ROPE_DIM
FFN_DIM = 2816
ROPE_THETA = 10000.0
RMS_EPS = 1e-6
Q_BLOCK = 128
F32 = jnp.float32

kernel_name = 'hybrid_conv_natten_mla_encoder'


def _rms_norm(x, g):
    xf = x.astype(F32)
    y = xf * lax.rsqrt(jnp.mean(xf * xf, axis=-1, keepdims=True) + RMS_EPS)
    return (y * g.astype(F32)).astype(x.dtype)


def _swiglu(x, w_gu, w_down):
    gate, up = jnp.split(x @ w_gu, 2, axis=-1)
    return (jax.nn.silu(gate) * up) @ w_down


def _rope(x, pos):
    half = x.shape[-1] // 2
    inv_freq = ROPE_THETA ** (-jnp.arange(half, dtype=F32) / half)
    ang = pos.astype(F32)[:, None] * inv_freq[None, :]
    cos = jnp.cos(ang)[None, :, None, :].astype(x.dtype)
    sin = jnp.sin(ang)[None, :, None, :].astype(x.dtype)
    x1, x2 = x[..., :half], x[..., half:]
    return jnp.concatenate([x1 * cos - x2 * sin, x1 * sin + x2 * cos], axis=-1)


def _short_conv(u, w):
    up = jnp.pad(u, ((0, 0), (1, 1), (0, 0)))
    return w[0] * up[:, :-2] + w[1] * up[:, 1:-1] + w[2] * up[:, 2:]


def _neighbourhood_attention(q, k, v, rpb, meta_bias):
    B, L, H, dh = q.shape
    T = L - N_META
    rows = T // GRID_W
    win_h = min(NA_MAX_WIN_H, rows)
    scale = dh ** -0.5
    q_m, k_m, v_m = q[:, :N_META], k[:, :N_META], v[:, :N_META]
    mb = meta_bias.astype(F32)[None, :, None, :]
    s_mm = jnp.einsum('bqhd,bmhd->bhqm', q_m, k_m).astype(F32) * scale + mb
    p_mm = jax.nn.softmax(s_mm, axis=-1).astype(v.dtype)
    o_meta = jnp.einsum('bhqm,bmhd->bqhd', p_mm, v_m)
    q_g = q[:, N_META:].reshape(B, rows, GRID_W, H, dh)
    k_g = k[:, N_META:].reshape(B, rows, GRID_W, H, dh)
    v_g = v[:, N_META:].reshape(B, rows, GRID_W, H, dh)
    col_start = np.clip(np.arange(GRID_W) - NA_WIN_W // 2, 0, GRID_W - NA_WIN_W)
    col_idx = col_start[:, None] + np.arange(NA_WIN_W)[None, :]
    col_rel = col_idx - np.arange(GRID_W)[:, None] + (NA_WIN_W - 1)
    rpb_c = rpb.astype(F32)[:, :, col_rel]
    n_loc = win_h * NA_WIN_W

    def row_block(args):
        r, q_row = args
        rs = jnp.clip(r - win_h // 2, 0, rows - win_h)
        k_rows = lax.dynamic_slice_in_dim(k_g, rs, win_h, axis=1)
        v_rows = lax.dynamic_slice_in_dim(v_g, rs, win_h, axis=1)
        k_win = k_rows[:, :, col_idx]
        v_win = v_rows[:, :, col_idx]
        row_rel = rs + jnp.arange(win_h) - r + (NA_MAX_WIN_H - 1)
        bias = jnp.take(rpb_c, row_rel, axis=1).transpose(0, 2, 1, 3)
        s_loc = jnp.einsum('bchd,bwcvhd->bhcwv', q_row, k_win).astype(F32) * scale + bias[None]
        s_loc = s_loc.reshape(B, H, GRID_W, n_loc)
        s_met = jnp.einsum('bchd,bmhd->bhcm', q_row, k_m).astype(F32) * scale + mb
        p = jax.nn.softmax(jnp.concatenate([s_loc, s_met], axis=-1), axis=-1).astype(v.dtype)
        p_loc = p[..., :n_loc].reshape(B, H, GRID_W, win_h, NA_WIN_W)
        p_met = p[..., n_loc:]
        return (jnp.einsum('bhcwv,bwcvhd->bchd', p_loc, v_win)
                + jnp.einsum('bhcm,bmhd->bchd', p_met, v_m))

    o_grid = lax.map(row_block, (jnp.arange(rows, dtype=jnp.int32), jnp.moveaxis(q_g, 1, 0)))
    o_grid = jnp.moveaxis(o_grid, 0, 1).reshape(B, T, H, dh)
    return jnp.concatenate([o_meta, o_grid], axis=1)


def _mla(q_lat, kv_lat, k_pe_raw, pos, q_norm, w_uq, kv_norm, w_ukv):
    B, L, _ = q_lat.shape
    q = (_rms_norm(q_lat, q_norm) @ w_uq).reshape(B, L, MLA_HEADS, MLA_NOPE_DIM + MLA_ROPE_DIM)
    q_nope = q[..., :MLA_NOPE_DIM]
    q_pe = _rope(q[..., MLA_NOPE_DIM:], pos)
    kv = (_rms_norm(kv_lat, kv_norm) @ w_ukv).reshape(B, L, MLA_HEADS, MLA_NOPE_DIM + MLA_V_DIM)
    k_nope = kv[..., :MLA_NOPE_DIM]
    v = kv[..., MLA_NOPE_DIM:]
    k_pe = _rope(k_pe_raw[:, :, None, :], pos)[:, :, 0]
    scale = (MLA_NOPE_DIM + MLA_ROPE_DIM) ** -0.5

    def attend(args):
        qn, qp = args
        s = (jnp.einsum('bqhd,bkhd->bhqk', qn, k_nope)
             + jnp.einsum('bqhd,bkd->bhqk', qp, k_pe)).astype(F32) * scale
        p = jax.nn.softmax(s, axis=-1).astype(v.dtype)
        return jnp.einsum('bhqk,bkhd->bqhd', p, v)

    o_meta = attend((q_nope[:, :N_META], q_pe[:, :N_META]))
    T = L - N_META
    nblk = T // Q_BLOCK
    qn_b = jnp.moveaxis(q_nope[:, N_META:].reshape(B, nblk, Q_BLOCK, MLA_HEADS, MLA_NOPE_DIM), 1, 0)
    qp_b = jnp.moveaxis(q_pe[:, N_META:].reshape(B, nblk, Q_BLOCK, MLA_HEADS, MLA_ROPE_DIM), 1, 0)
    o_b = lax.map(attend, (qn_b, qp_b))
    o_real = jnp.moveaxis(o_b, 0, 1).reshape(B, T, MLA_HEADS, MLA_V_DIM)
    return jnp.concatenate([o_meta, o_real], axis=1).reshape(B, L, MLA_DIM)


SPLIT_POINTS = [int(i) for i in np.cumsum([CONV_DIM] * 3 + [NA_DIM] * 3 + [MLA_Q_RANK, MLA_KV_RANK])]


def _mixer(h, pos, w_in, conv_w, na_rpb, na_meta_bias, mla_q_norm, mla_w_uq, mla_kv_norm, mla_w_ukv,
           conv_out_norm, na_out_norm, mla_out_norm, w_o):
    B, L, _ = h.shape
    z = h @ w_in
    cb, cc, cu, nq, nk, nv, q_lat, kv_lat, k_pe_raw = jnp.split(z, SPLIT_POINTS, axis=-1)
    y_conv = cb * _short_conv(cc * cu, conv_w)
    shp = (B, L, NA_HEADS, NA_HEAD_DIM)
    y_na = _neighbourhood_attention(nq.reshape(shp), nk.reshape(shp), nv.reshape(shp),
                                    na_rpb, na_meta_bias).reshape(B, L, NA_DIM)
    y_mla = _mla(q_lat, kv_lat, k_pe_raw, pos, mla_q_norm, mla_w_uq, mla_kv_norm, mla_w_ukv)
    y = jnp.concatenate([_rms_norm(y_conv, conv_out_norm), _rms_norm(y_na, na_out_norm),
                         _rms_norm(y_mla, mla_out_norm)], axis=-1)
    return y @ w_o


def _layer(h, pos, w):
    (f1_pre, f1_gu, f1_down, f1_post, m_pre, w_in, conv_w, na_rpb, na_meta_bias, q_norm, w_uq,
     kv_norm, w_ukv, c_on, n_on, m_on, w_o, m_post, f2_pre, f2_gu, f2_down, f2_post) = w
    h = h + 0.5 * _rms_norm(_swiglu(_rms_norm(h, f1_pre), f1_gu, f1_down), f1_post)
    h = h + _rms_norm(_mixer(_rms_norm(h, m_pre), pos, w_in, conv_w, na_rpb, na_meta_bias, q_norm, w_uq,
                             kv_norm, w_ukv, c_on, n_on, m_on, w_o), m_post)
    h = h + 0.5 * _rms_norm(_swiglu(_rms_norm(h, f2_pre), f2_gu, f2_down), f2_post)
    return h


def _trunk(x, meta_tokens, weights):
    B, T, _ = x.shape
    L = N_META + T
    meta = jnp.broadcast_to(meta_tokens.astype(x.dtype)[None], (B, N_META, D_MODEL))
    h = jnp.concatenate([meta, x], axis=1)
    pos = jnp.arange(L, dtype=jnp.int32)
    for l in range(DEPTH):
        h = _layer(h, pos, tuple(wt[l] for wt in weights))
    return h[:, N_META:]


def setup_inputs(seed: int = 0) -> dict:
    key = jax.random.key(seed)
    ks = jax.random.split(key, 32)

    def w(k, shape, fan_in):
        return jax.random.normal(k, shape, F32) * fan_in ** -0.5

    def g(k, shape):
        return 1.0 + 0.05 * jax.random.normal(k, shape, F32)

    D = DEPTH
    return {
        'x_prompt': jax.random.normal(ks[0], (BATCH, SEQ, D_MODEL), F32),
        'x_sample': jax.random.normal(ks[1], (DEC_BATCH, DEC_SEQ, D_MODEL), F32),
        'meta_tokens': jax.random.normal(ks[2], (N_META, D_MODEL), F32),
        'ffn1_pre_norm': g(ks[3], (D, D_MODEL)),
        'ffn1_w_gu': w(ks[4], (D, D_MODEL, 2 * FFN_DIM), D_MODEL),
        'ffn1_w_down': w(ks[5], (D, FFN_DIM, D_MODEL), FFN_DIM),
        'ffn1_post_norm': g(ks[6], (D, D_MODEL)),
        'mix_pre_norm': g(ks[7], (D, D_MODEL)),
        'w_in': w(ks[8], (D, D_MODEL, IN_PROJ_DIM), D_MODEL),
        'conv_w': w(ks[9], (D, CONV_WIDTH, CONV_DIM), CONV_WIDTH),
        'na_rpb': 0.1 * jax.random.normal(ks[10], (D, NA_HEADS, 2 * NA_MAX_WIN_H - 1, 2 * NA_WIN_W - 1), F32),
        'na_meta_bias': 0.1 * jax.random.normal(ks[11], (D, NA_HEADS, N_META), F32),
        'mla_q_norm': g(ks[12], (D, MLA_Q_RANK)),
        'mla_w_uq': w(ks[13], (D, MLA_Q_RANK, MLA_HEADS * (MLA_NOPE_DIM + MLA_ROPE_DIM)), MLA_Q_RANK),
        'mla_kv_norm': g(ks[14], (D, MLA_KV_RANK)),
        'mla_w_ukv': w(ks[15], (D, MLA_KV_RANK, MLA_HEADS * (MLA_NOPE_DIM + MLA_V_DIM)), MLA_KV_RANK),
        'conv_out_norm': g(ks[16], (D, CONV_DIM)),
        'na_out_norm': g(ks[17], (D, NA_DIM)),
        'mla_out_norm': g(ks[18], (D, MLA_DIM)),
        'w_o': w(ks[19], (D, MIX_DIM, D_MODEL), MIX_DIM),
        'mix_post_norm': g(ks[20], (D, D_MODEL)),
        'ffn2_pre_norm': g(ks[21], (D, D_MODEL)),
        'ffn2_w_gu': w(ks[22], (D, D_MODEL, 2 * FFN_DIM), D_MODEL),
        'ffn2_w_down': w(ks[23], (D, FFN_DIM, D_MODEL), FFN_DIM),
        'ffn2_post_norm': g(ks[24], (D, D_MODEL)),
    }


def reference(x_prompt, x_sample, meta_tokens, ffn1_pre_norm, ffn1_w_gu, ffn1_w_down, ffn1_post_norm,
              mix_pre_norm, w_in, conv_w, na_rpb, na_meta_bias, mla_q_norm, mla_w_uq, mla_kv_norm, mla_w_ukv,
              conv_out_norm, na_out_norm, mla_out_norm, w_o, mix_post_norm, ffn2_pre_norm, ffn2_w_gu,
              ffn2_w_down, ffn2_post_norm):
    weights = (ffn1_pre_norm, ffn1_w_gu, ffn1_w_down, ffn1_post_norm, mix_pre_norm, w_in, conv_w, na_rpb,
               na_meta_bias, mla_q_norm, mla_w_uq, mla_kv_norm, mla_w_ukv, conv_out_norm, na_out_norm,
               mla_out_norm, w_o, mix_post_norm, ffn2_pre_norm, ffn2_w_gu, ffn2_w_down, ffn2_post_norm)
    y_prompt = _trunk(x_prompt, meta_tokens, weights)
    y_sample = _trunk(x_sample, meta_tokens, weights)
    return (y_prompt, y_sample)
```

```python
import functools
import math

import numpy as np
import jax
import jax.numpy as jnp
from jax import lax
from jax.experimental import pallas as pl
from jax.experimental.pallas import tpu as pltpu

F32 = jnp.float32
BF16 = jnp.bfloat16

D_MODEL = 1024
DEPTH = 4
N_META = 16
GRID_W = 64
CONV_DIM = 256
NA_HEADS = 4
NA_HEAD_DIM = 64
NA_DIM = NA_HEADS * NA_HEAD_DIM
NA_MAX_WIN_H = 8
NA_WIN_W = 16
MLA_HEADS = 8
MLA_NOPE_DIM = 64
MLA_ROPE_DIM = 32
MLA_V_DIM = 64
MLA_Q_RANK = 768
MLA_KV_RANK = 256
MLA_DIM = MLA_HEADS * MLA_V_DIM
FFN_DIM = 2816
ROPE_THETA = 10000.0
RMS_EPS = 1e-6

LANES = 128
HEAD_SLOT = 128
IN_COLS = 6 * 256 + MLA_Q_RANK + MLA_KV_RANK + LANES
FFN_CHUNK = 256
ROW_TILE = 512
NA_QROWS = 2
NA_KROWS = NA_MAX_WIN_H + NA_QROWS - 1
NA_QB = NA_QROWS * GRID_W
NA_KB = NA_KROWS * GRID_W
MLA_TQ = 256
MLA_CK = 512
MASK_VALUE = -1e30
VMEM_LIMIT = 56 * 1024 * 1024


def _rms(x, g):
    ms = jnp.mean(x * x, axis=-1, keepdims=True)
    return x * lax.rsqrt(ms + RMS_EPS) * g


def _dot(a, b):
    return jnp.dot(a, b, preferred_element_type=F32)


def _dot_nt(a, b):
    return lax.dot_general(a, b, (((1,), (1,)), ((), ())), preferred_element_type=F32)


def _const_spec(block_shape, index):
    return pl.BlockSpec(block_shape, lambda *_: index, pipeline_mode=pl.Buffered(1))


def _params(n_axes):
    return pltpu.CompilerParams(dimension_semantics=("arbitrary",) * n_axes,
                                vmem_limit_bytes=VMEM_LIMIT)


def _ffn_kernel(h_ref, pre_ref, wgu_ref, wd_ref, post_ref, o_ref):
    x = h_ref[...]
    xn = _rms(x, pre_ref[...]).astype(BF16)
    acc = None
    for c in range(FFN_DIM // FFN_CHUNK):
        lo = c * FFN_CHUNK
        g = _dot(xn, wgu_ref[:, lo:lo + FFN_CHUNK])
        u = _dot(xn, wgu_ref[:, FFN_DIM + lo:FFN_DIM + lo + FFN_CHUNK])
        a = (g * jax.nn.sigmoid(g) * u).astype(BF16)
        d = _dot(a, wd_ref[lo:lo + FFN_CHUNK, :])
        acc = d if acc is None else acc + d
    o_ref[...] = x + 0.5 * _rms(acc, post_ref[...])


def _ffn(h, layer, pre, wgu, wd, post, tm):
    n = h.shape[0]
    row = pl.BlockSpec((tm, D_MODEL), lambda i: (i, 0))
    return pl.pallas_call(
        _ffn_kernel,
        out_shape=jax.ShapeDtypeStruct(h.shape, F32),
        grid=(n // tm,),
        in_specs=[row,
                  _const_spec((None, 1, D_MODEL), (layer, 0, 0)),
                  _const_spec((None, D_MODEL, 2 * FFN_DIM), (layer, 0, 0)),
                  _const_spec((None, FFN_DIM, D_MODEL), (layer, 0, 0)),
                  _const_spec((None, 1, D_MODEL), (layer, 0, 0))],
        out_specs=row,
        compiler_params=_params(1),
        name="ffn",
    )(h, pre, wgu, wd, post)


def _inproj_kernel(h_ref, pre_ref, win_ref, qn_ref, wuq_ref, kvn_ref, wuk_ref, wuv_ref, te_ref,
                   cb_ref, ccu_ref, nq_ref, nk_ref, nv_ref, q_ref, k_ref, v_ref):
    tm = h_ref.shape[0]
    xn = _rms(h_ref[...], pre_ref[...]).astype(BF16)
    z = _dot(xn, win_ref[...])
    cb_ref[...] = z[:, 0:256]
    ccu_ref[...] = z[:, 256:512] * z[:, 512:768]
    nq_ref[...] = (z[:, 768:1024] * (NA_HEAD_DIM ** -0.5)).astype(BF16)
    nk_ref[...] = z[:, 1024:1280].astype(BF16)
    nv_ref[...] = z[:, 1280:1536].astype(BF16)
    ql = _rms(z[:, 1536:1536 + MLA_Q_RANK], qn_ref[...]).astype(BF16)
    qf = _dot(ql, wuq_ref[...])
    kvl = _rms(z[:, 2304:2304 + MLA_KV_RANK], kvn_ref[...]).astype(BF16)
    kn = _dot(kvl, wuk_ref[...])
    v_ref[...] = _dot(kvl, wuv_ref[...]).astype(BF16)
    te = te_ref[...]
    y = z[:, 2560:2560 + LANES] * te
    kr = y + pltpu.roll(y, MLA_ROPE_DIM, axis=1)
    low = lax.broadcasted_iota(jnp.int32, (tm, LANES), 1) < MLA_NOPE_DIM
    for h in range(MLA_HEADS):
        sl = slice(h * HEAD_SLOT, (h + 1) * HEAD_SLOT)
        qh = qf[:, sl]
        qt = qh * te
        q_ref[:, sl] = (jnp.where(low, qh, qt) if h % 2 == 0 else jnp.where(low, qt, qh)).astype(BF16)
    for j in range(MLA_HEADS // 2):
        kp = kn[:, j * LANES:(j + 1) * LANES]
        k_ref[:, (2 * j) * HEAD_SLOT:(2 * j + 1) * HEAD_SLOT] = jnp.where(low, kp, kr).astype(BF16)
        k_ref[:, (2 * j + 1) * HEAD_SLOT:(2 * j + 2) * HEAD_SLOT] = jnp.where(low, kr, kp).astype(BF16)


def _inproj(h, layer, w, te, te_blocks, tm):
    n = h.shape[0]
    row = lambda c: pl.BlockSpec((tm, c), lambda i: (i, 0))
    outs = [(256, F32), (256, F32), (256, BF16), (256, BF16), (256, BF16),
            (MLA_HEADS * HEAD_SLOT, BF16), (MLA_HEADS * HEAD_SLOT, BF16), (MLA_DIM, BF16)]
    return pl.pallas_call(
        _inproj_kernel,
        out_shape=[jax.ShapeDtypeStruct((n, c), dt) for c, dt in outs],
        grid=(n // tm,),
        in_specs=[row(D_MODEL),
                  _const_spec((None, 1, D_MODEL), (layer, 0, 0)),
                  _const_spec((None, D_MODEL, IN_COLS), (layer, 0, 0)),
                  _const_spec((None, 1, MLA_Q_RANK), (layer, 0, 0)),
                  _const_spec((None, MLA_Q_RANK, MLA_HEADS * HEAD_SLOT), (layer, 0, 0)),
                  _const_spec((None, 1, MLA_KV_RANK), (layer, 0, 0)),
                  _const_spec((None, MLA_KV_RANK, MLA_HEADS * MLA_NOPE_DIM), (layer, 0, 0)),
                  _const_spec((None, MLA_KV_RANK, MLA_DIM), (layer, 0, 0)),
                  pl.BlockSpec((tm, LANES), lambda i: (i % te_blocks, 0))],
        out_specs=[row(c) for c, _ in outs],
        compiler_params=_params(1),
        name="inproj",
    )(h, w["mix_pre"], w["w_in"], w["q_norm"], w["w_uq"], w["kv_norm"], w["w_uk"], w["w_uv"], te)


def _softmax_pv(s_list, v_list):
    m = s_list[0].max(axis=-1, keepdims=True)
    for s in s_list[1:]:
        m = jnp.maximum(m, s.max(axis=-1, keepdims=True))
    den = None
    out = None
    for s, v in zip(s_list, v_list):
        p = jnp.exp(s - m)
        l = p.sum(axis=-1, keepdims=True)
        o = _dot(p.astype(BF16), v)
        den = l if den is None else den + l
        out = o if out is None else out + o
    return out / den


def _na_kernel(vid_ref, st_ref, q_ref, k_ref, v_ref, km_ref, vm_ref, bias_ref, mb_ref, o_ref):
    del vid_ref
    i = pl.program_id(1)
    start = pl.multiple_of(st_ref[i] * GRID_W, GRID_W)
    low = lax.broadcasted_iota(jnp.int32, (NA_QB, LANES), 1) < NA_HEAD_DIM
    for j in range(NA_HEADS // 2):
        sl = slice(j * LANES, (j + 1) * LANES)
        qp = q_ref[:, sl]
        kw = k_ref[pl.ds(start, NA_KB), sl]
        vw = v_ref[pl.ds(start, NA_KB), sl]
        km = km_ref[:, sl]
        vm = vm_ref[:, sl]
        halves = []
        for half in range(2):
            h = 2 * j + half
            qh = jnp.where(low if half == 0 else jnp.logical_not(low), qp, jnp.zeros_like(qp))
            s = _dot_nt(qh, kw) + bias_ref[h]
            sm = _dot_nt(qh, km) + mb_ref[h:h + 1, :]
            halves.append(_softmax_pv([s, sm], [vw, vm]))
        o_ref[:, sl] = jnp.where(low, halves[0], halves[1])


def _na(nq, nk, nv, nkm, nvm, bias, mb, layer, vid, st, batch, seq, meta_off):
    nblk = seq // NA_QB
    n_patterns = bias.shape[0] // DEPTH
    qspec = pl.BlockSpec((NA_QB, NA_DIM), lambda b, i, vid, st: (b * nblk + i, 0))
    kvspec = pl.BlockSpec((seq, NA_DIM), lambda b, i, vid, st: (b, 0))
    mspec = pl.BlockSpec((N_META, NA_DIM), lambda b, i, vid, st: (meta_off + b, 0))
    grid_spec = pltpu.PrefetchScalarGridSpec(
        num_scalar_prefetch=2,
        grid=(batch, nblk),
        in_specs=[qspec, kvspec, kvspec, mspec, mspec,
                  pl.BlockSpec((None, NA_HEADS, NA_QB, NA_KB),
                               lambda b, i, vid, st: (layer * n_patterns + vid[i], 0, 0, 0)),
                  pl.BlockSpec((None, NA_HEADS, N_META), lambda b, i, vid, st: (layer, 0, 0))],
        out_specs=qspec)
    return pl.pallas_call(
        _na_kernel,
        out_shape=jax.ShapeDtypeStruct((batch * seq, NA_DIM), F32),
        grid_spec=grid_spec,
        compiler_params=_params(2),
        name="na",
    )(vid, st, nq, nk, nv, nkm, nvm, bias, mb)


def _na_meta_kernel(q_ref, k_ref, v_ref, mb_ref, o_ref):
    low = lax.broadcasted_iota(jnp.int32, (N_META, LANES), 1) < NA_HEAD_DIM
    for j in range(NA_HEADS // 2):
        sl = slice(j * LANES, (j + 1) * LANES)
        qp = q_ref[:, sl]
        halves = []
        for half in range(2):
            h = 2 * j + half
            qh = jnp.where(low if half == 0 else jnp.logical_not(low), qp, jnp.zeros_like(qp))
            s = _dot_nt(qh, k_ref[:, sl]) + mb_ref[h:h + 1, :]
            halves.append(_softmax_pv([s], [v_ref[:, sl]]))
        o_ref[:, sl] = jnp.where(low, halves[0], halves[1])


def _na_meta(nqm, nkm, nvm, mb, layer):
    n = nqm.shape[0]
    spec = pl.BlockSpec((N_META, NA_DIM), lambda b: (b, 0))
    return pl.pallas_call(
        _na_meta_kernel,
        out_shape=jax.ShapeDtypeStruct((n, NA_DIM), F32),
        grid=(n // N_META,),
        in_specs=[spec, spec, spec, pl.BlockSpec((None, NA_HEADS, N_META), lambda b: (layer, 0, 0))],
        out_specs=spec,
        compiler_params=_params(1),
        name="na_meta",
    )(nqm, nkm, nvm, mb)


_MLA_EXP2_SCALE = (MLA_NOPE_DIM + MLA_ROPE_DIM) ** -0.5 * math.log2(math.e)


def _mla_kernel(q_ref, k_ref, v_ref, km_ref, vm_ref, o_ref, *, seq):
    tq = q_ref.shape[0]
    low = lax.broadcasted_iota(jnp.int32, (tq, LANES), 1) < MLA_V_DIM
    c = _MLA_EXP2_SCALE
    for j in range(MLA_HEADS // 2):
        vsl = slice(j * LANES, (j + 1) * LANES)
        halves = []
        for half in range(2):
            h = 2 * j + half
            hsl = slice(h * HEAD_SLOT, (h + 1) * HEAD_SLOT)
            q = q_ref[:, hsl]
            sm = _dot_nt(q, km_ref[:, hsl])
            m0 = sm.max(axis=-1, keepdims=True)
            p0 = jnp.exp2((sm - m0) * c)
            l0 = p0.sum(axis=-1, keepdims=True)
            a0 = _dot(p0.astype(BF16), vm_ref[:, vsl])

            def body(ci, carry, q=q, hsl=hsl, vsl=vsl):
                m, l, acc = carry
                off = pl.multiple_of(ci * MLA_CK, MLA_CK)
                s = _dot_nt(q, k_ref[pl.ds(off, MLA_CK), hsl])
                mn = jnp.maximum(m, s.max(axis=-1, keepdims=True))
                alpha = jnp.exp2((m - mn) * c)
                p = jnp.exp2((s - mn) * c)
                l = alpha * l + p.sum(axis=-1, keepdims=True)
                acc = alpha * acc + _dot(p.astype(BF16), v_ref[pl.ds(off, MLA_CK), vsl])
                return mn, l, acc

            _, l, acc = lax.fori_loop(0, seq // MLA_CK, body, (m0, l0, a0))
            halves.append(acc / l)
        o_ref[:, vsl] = jnp.where(low, halves[0], halves[1])


def _mla(q, k, v, km, vm, batch, seq, meta_off, tq, q_rows_per_batch, q_off):
    nq = q_rows_per_batch // tq
    qspec = lambda c: pl.BlockSpec((tq, c), lambda b, i: (q_off + b * nq + i, 0))
    kvspec = lambda c: pl.BlockSpec((seq, c), lambda b, i: (b, 0))
    mspec = lambda c: pl.BlockSpec((N_META, c), lambda b, i: (meta_off + b, 0))
    ospec = pl.BlockSpec((tq, MLA_DIM), lambda b, i: (b * nq + i, 0))
    return pl.pallas_call(
        functools.partial(_mla_kernel, seq=seq),
        out_shape=jax.ShapeDtypeStruct((batch * q_rows_per_batch, MLA_DIM), F32),
        grid=(batch, nq),
        in_specs=[qspec(MLA_HEADS * HEAD_SLOT), kvspec(MLA_HEADS * HEAD_SLOT), kvspec(MLA_DIM),
                  mspec(MLA_HEADS * HEAD_SLOT), mspec(MLA_DIM)],
        out_specs=ospec,
        compiler_params=_params(2),
        name="mla",
    )(q, k, v, km, vm)


def _out_kernel(h_ref, cb_ref, ccu_ref, halo_a_ref, halo_b_ref, halo_c_ref, yna_ref, ymla_ref,
                convw_ref, cn_ref, nn_ref, mn_ref, wo_ref, post_ref, o_ref, *, meta, tiles_per_seq,
                prompt_batches):
    tm = h_ref.shape[0]
    i = pl.program_id(0)
    ccu = ccu_ref[...]
    if meta:
        prev_row = jnp.zeros((1, CONV_DIM), F32)
        next_row = jnp.where(i < prompt_batches, halo_a_ref[0:1, :], halo_b_ref[0:1, :])
    else:
        t = i % tiles_per_seq
        prev_row = jnp.where(t == 0, halo_c_ref[N_META - 1:N_META, :], halo_a_ref[7:8, :])
        next_row = jnp.where(t == tiles_per_seq - 1, jnp.zeros((1, CONV_DIM), F32), halo_b_ref[0:1, :])
    row = lax.broadcasted_iota(jnp.int32, (tm, CONV_DIM), 0)
    dn = jnp.where(row == 0, prev_row, pltpu.roll(ccu, 1, axis=0))
    up = jnp.where(row == tm - 1, next_row, pltpu.roll(ccu, tm - 1, axis=0))
    w = convw_ref[...]
    yc = cb_ref[...] * (w[0:1, :] * dn + w[1:2, :] * ccu + w[2:3, :] * up)
    y = jnp.concatenate([_rms(yc, cn_ref[...]), _rms(yna_ref[...], nn_ref[...]),
                         _rms(ymla_ref[...], mn_ref[...])], axis=-1).astype(BF16)
    o_ref[...] = h_ref[...] + _rms(_dot(y, wo_ref[...]), post_ref[...])


def _out(h, cb, ccu, halo_a, halo_b, halo_c, yna, ymla, layer, w, *, meta, tm, seq=None,
         meta_off=0, prompt_batches=0, seq_a=0, seq_b=0):
    n = h.shape[0]
    row = lambda c: pl.BlockSpec((tm, c), lambda i: (i, 0))
    if meta:
        tiles_per_seq = 1
        a_spec = pl.BlockSpec((8, CONV_DIM), lambda i: (jnp.minimum(i, prompt_batches - 1) * (seq_a // 8), 0))
        b_spec = pl.BlockSpec((8, CONV_DIM), lambda i: (jnp.maximum(i - prompt_batches, 0) * (seq_b // 8), 0))
        c_spec = pl.BlockSpec((8, CONV_DIM), lambda i: (0, 0))
    else:
        tiles_per_seq = seq // tm
        last = n // 8 - 1
        a_spec = pl.BlockSpec((8, CONV_DIM), lambda i: (jnp.maximum(i * (tm // 8) - 1, 0), 0))
        b_spec = pl.BlockSpec((8, CONV_DIM), lambda i: (jnp.minimum((i + 1) * (tm // 8), last), 0))
        c_spec = pl.BlockSpec((N_META, CONV_DIM), lambda i: (meta_off + i // tiles_per_seq, 0))
    return pl.pallas_call(
        functools.partial(_out_kernel, meta=meta, tiles_per_seq=tiles_per_seq, prompt_batches=prompt_batches),
        out_shape=jax.ShapeDtypeStruct(h.shape, F32),
        grid=(n // tm,),
        in_specs=[row(D_MODEL), row(CONV_DIM), row(CONV_DIM), a_spec, b_spec, c_spec,
                  row(NA_DIM), row(MLA_DIM),
                  _const_spec((None, 3, CONV_DIM), (layer, 0, 0)),
                  _const_spec((None, 1, CONV_DIM), (layer, 0, 0)),
                  _const_spec((None, 1, NA_DIM), (layer, 0, 0)),
                  _const_spec((None, 1, MLA_DIM), (layer, 0, 0)),
                  _const_spec((None, D_MODEL, D_MODEL), (layer, 0, 0)),
                  _const_spec((None, 1, D_MODEL), (layer, 0, 0))],
        out_specs=row(D_MODEL),
        compiler_params=_params(1),
        name="mix_out",
    )(h, cb, ccu, halo_a, halo_b, halo_c, yna, ymla,
      w["conv_w"], w["conv_on"], w["na_on"], w["mla_on"], w["w_o"], w["mix_post"])


def _na_patterns():
    patterns = []
    per_rows = {}
    q = np.arange(NA_QB)
    k = np.arange(NA_KB)
    qc, kc = q % GRID_W, k % GRID_W
    for rows in (2048 // GRID_W, 4096 // GRID_W):
        win_h = min(NA_MAX_WIN_H, rows)
        vids, starts = [], []
        for blk in range(rows // NA_QROWS):
            r0 = blk * NA_QROWS
            start = int(np.clip(r0 - win_h // 2, 0, rows - NA_KROWS))
            qr = r0 + q // GRID_W
            kr = start + k // GRID_W
            rs = np.clip(qr - win_h // 2, 0, rows - win_h)
            cs = np.clip(qc - NA_WIN_W // 2, 0, GRID_W - NA_WIN_W)
            valid = ((kr[None] >= rs[:, None]) & (kr[None] < rs[:, None] + win_h)
                     & (kc[None] >= cs[:, None]) & (kc[None] < cs[:, None] + NA_WIN_W))
            assert (valid.sum(axis=1) == win_h * NA_WIN_W).all()
            idx = (kr[None] - qr[:, None] + NA_MAX_WIN_H - 1) * (2 * NA_WIN_W - 1) \
                + (kc[None] - qc[:, None] + NA_WIN_W - 1)
            idx = np.where(valid, idx, 0).astype(np.int32)
            for n, (pv, pi) in enumerate(patterns):
                if np.array_equal(pv, valid) and np.array_equal(pi, idx):
                    vids.append(n)
                    break
            else:
                vids.append(len(patterns))
                patterns.append((valid, idx))
            starts.append(start)
        per_rows[rows] = (np.asarray(vids, np.int32), np.asarray(starts, np.int32))
    valid = np.stack([p[0] for p in patterns])
    idx = np.stack([p[1] for p in patterns])
    return valid, idx, per_rows


_NA_VALID, _NA_IDX, _NA_BLOCKS = _na_patterns()


def _rope_table(pos):
    half = MLA_ROPE_DIM // 2
    inv_freq = ROPE_THETA ** (-jnp.arange(half, dtype=F32) / half)
    ang = pos.astype(F32)[:, None] * inv_freq[None, :]
    cos = jnp.concatenate([jnp.cos(ang)] * 2, axis=-1)
    sin = jnp.concatenate([jnp.sin(ang)] * 2, axis=-1)
    return jnp.concatenate([cos, sin, cos, sin], axis=-1)


def _rot_cols(w):
    half = MLA_ROPE_DIM // 2
    return jnp.concatenate([-w[..., half:], w[..., :half]], axis=-1)


def _prep_weights(ffn1_pre, ffn1_gu, ffn1_down, ffn1_post, mix_pre, w_in, conv_w, na_rpb, na_mb, q_norm, w_uq,
                  kv_norm, w_ukv, conv_on, na_on, mla_on, w_o, mix_post, ffn2_pre, ffn2_gu, ffn2_down, ffn2_post):
    vec = lambda g: g[:, None, :]
    kpe = w_in[:, :, 2560:2560 + MLA_ROPE_DIM]
    kpe_rot = _rot_cols(kpe)
    w_in_ext = jnp.concatenate([w_in[:, :, :2560], kpe, kpe_rot, kpe, kpe_rot], axis=-1).astype(BF16)
    per_head = w_uq.reshape(DEPTH, MLA_Q_RANK, MLA_HEADS, MLA_NOPE_DIM + MLA_ROPE_DIM)
    nope, pe = per_head[..., :MLA_NOPE_DIM], per_head[..., MLA_NOPE_DIM:]
    even = jnp.concatenate([nope, pe, _rot_cols(pe)], axis=-1)
    odd = jnp.concatenate([pe, _rot_cols(pe), nope], axis=-1)
    is_even = (jnp.arange(MLA_HEADS) % 2 == 0)[None, None, :, None]
    w_uq_ext = jnp.where(is_even, even, odd).reshape(DEPTH, MLA_Q_RANK, MLA_HEADS * HEAD_SLOT).astype(BF16)
    kv_heads = w_ukv.reshape(DEPTH, MLA_KV_RANK, MLA_HEADS, MLA_NOPE_DIM + MLA_V_DIM)
    w_uk = kv_heads[..., :MLA_NOPE_DIM].reshape(DEPTH, MLA_KV_RANK, MLA_HEADS * MLA_NOPE_DIM).astype(BF16)
    w_uv = kv_heads[..., MLA_NOPE_DIM:].reshape(DEPTH, MLA_KV_RANK, MLA_DIM).astype(BF16)
    rpb_flat = na_rpb.reshape(DEPTH, NA_HEADS, -1)
    bias = jnp.where(_NA_VALID[None, None], rpb_flat[:, :, _NA_IDX], MASK_VALUE)
    bias = bias.transpose(0, 2, 1, 3, 4).astype(F32).reshape(-1, NA_HEADS, NA_QB, NA_KB)
    return dict(
        ffn1=(vec(ffn1_pre), ffn1_gu.astype(BF16), ffn1_down.astype(BF16), vec(ffn1_post)),
        ffn2=(vec(ffn2_pre), ffn2_gu.astype(BF16), ffn2_down.astype(BF16), vec(ffn2_post)),
        mix_pre=vec(mix_pre), w_in=w_in_ext, q_norm=vec(q_norm), w_uq=w_uq_ext, kv_norm=vec(kv_norm),
        w_uk=w_uk, w_uv=w_uv, conv_w=conv_w, conv_on=vec(conv_on), na_on=vec(na_on), mla_on=vec(mla_on),
        w_o=w_o.astype(BF16), mix_post=vec(mix_post), na_bias=bias, na_mb=na_mb)


def kernel(x_prompt, x_sample, meta_tokens, ffn1_pre_norm, ffn1_w_gu, ffn1_w_down, ffn1_post_norm, mix_pre_norm,
           w_in, conv_w, na_rpb, na_meta_bias, mla_q_norm, mla_w_uq, mla_kv_norm, mla_w_ukv, conv_out_norm,
           na_out_norm, mla_out_norm, w_o, mix_post_norm, ffn2_pre_norm, ffn2_w_gu, ffn2_w_down, ffn2_post_norm):
    w = _prep_weights(ffn1_pre_norm, ffn1_w_gu, ffn1_w_down, ffn1_post_norm, mix_pre_norm, w_in, conv_w, na_rpb,
                      na_meta_bias, mla_q_norm, mla_w_uq, mla_kv_norm, mla_w_ukv, conv_out_norm, na_out_norm,
                      mla_out_norm, w_o, mix_post_norm, ffn2_pre_norm, ffn2_w_gu, ffn2_w_down, ffn2_post_norm)
    groups = []
    for x in (x_prompt, x_sample):
        b, t, _ = x.shape
        groups.append(dict(batch=b, seq=t, te=_rope_table(N_META + jnp.arange(t))))
    nb = [g["batch"] for g in groups]
    meta_off = [0, nb[0]]
    n_meta_rows = sum(nb) * N_META
    te_meta = jnp.tile(_rope_table(jnp.arange(N_META)), (sum(nb), 1))
    hs = [x_prompt.reshape(-1, D_MODEL), x_sample.reshape(-1, D_MODEL),
          jnp.tile(meta_tokens.astype(F32), (sum(nb), 1))]
    tms = [ROW_TILE, ROW_TILE, n_meta_rows]

    for layer in range(DEPTH):
        hs = [_ffn(h, layer, *w["ffn1"], tm) for h, tm in zip(hs, tms)]
        proj = [_inproj(hs[g], layer, w, groups[g]["te"], groups[g]["seq"] // ROW_TILE, ROW_TILE) for g in range(2)]
        proj.append(_inproj(hs[2], layer, w, te_meta, 1, n_meta_rows))
        cb_m, ccu_m, nq_m, nk_m, nv_m, q_m, k_m, v_m = proj[2]
        yna_m = _na_meta(nq_m, nk_m, nv_m, w["na_mb"], layer)
        new_hs, ymla_m = [], []
        for g in range(2):
            cb, ccu, nq, nk, nv, q, k, v = proj[g]
            b, t = groups[g]["batch"], groups[g]["seq"]
            vid, st = _NA_BLOCKS[t // GRID_W]
            yna = _na(nq, nk, nv, nk_m, nv_m, w["na_bias"], w["na_mb"], layer,
                      jnp.asarray(vid), jnp.asarray(st), b, t, meta_off[g])
            ymla = _mla(q, k, v, k_m, v_m, b, t, meta_off[g], MLA_TQ, t, 0)
            ymla_m.append(_mla(q_m, k, v, k_m, v_m, b, t, meta_off[g], N_META, N_META, meta_off[g]))
            new_hs.append(_out(hs[g], cb, ccu, ccu, ccu, ccu_m, yna, ymla, layer, w, meta=False, tm=ROW_TILE,
                               seq=t, meta_off=meta_off[g]))
        new_hs.append(_out(hs[2], cb_m, ccu_m, proj[0][1], proj[1][1], ccu_m, yna_m, jnp.concatenate(ymla_m, axis=0),
                           layer, w, meta=True, tm=N_META, prompt_batches=nb[0],
                           seq_a=groups[0]["seq"], seq_b=groups[1]["seq"]))
        hs = new_hs
        last = layer == DEPTH - 1
        hs = [h if (last and n == 2) else _ffn(h, layer, *w["ffn2"], tm) for n, (h, tm) in enumerate(zip(hs, tms))]
    return (hs[0].reshape(x_prompt.shape), hs[1].reshape(x_sample.shape))
```

```python
import functools
import math

import numpy as np
import jax
import jax.numpy as jnp
from jax import lax
from jax.experimental import pallas as pl
from jax.experimental.pallas import tpu as pltpu

F32 = jnp.float32
BF16 = jnp.bfloat16

D_MODEL = 1024
DEPTH = 4
N_META = 16
GRID_W = 64
CONV_DIM = 256
NA_HEADS = 4
NA_HEAD_DIM = 64
NA_DIM = NA_HEADS * NA_HEAD_DIM
NA_MAX_WIN_H = 8
NA_WIN_W = 16
MLA_HEADS = 8
MLA_NOPE_DIM = 64
MLA_ROPE_DIM = 32
MLA_V_DIM = 64
MLA_Q_RANK = 768
MLA_KV_RANK = 256
MLA_DIM = MLA_HEADS * MLA_V_DIM
FFN_DIM = 2816
ROPE_THETA = 10000.0
RMS_EPS = 1e-6

LANES = 128
HEAD_SLOT = 128
IN_COLS = 6 * 256 + MLA_Q_RANK + MLA_KV_RANK + LANES
FFN_CHUNK = 256
ROW_TILE = 512
NA_QROWS = 2
NA_KROWS = NA_MAX_WIN_H + NA_QROWS - 1
NA_QB = NA_QROWS * GRID_W
NA_KB = NA_KROWS * GRID_W
NA_KCAT = 640
NA_UNROLL = 2
MLA_TQ = 256
MLA_CK = 1024
MASK_VALUE = -1e30
VMEM_LIMIT = 56 * 1024 * 1024


def _rms(x, g):
    ms = jnp.mean(x * x, axis=-1, keepdims=True)
    return x * lax.rsqrt(ms + RMS_EPS) * g


def _dot(a, b):
    return jnp.dot(a, b, preferred_element_type=F32)


def _dot_nt(a, b):
    return lax.dot_general(a, b, (((1,), (1,)), ((), ())), preferred_element_type=F32)


def _const_spec(block_shape, index):
    return pl.BlockSpec(block_shape, lambda *_: index, pipeline_mode=pl.Buffered(1))


def _params(n_axes, flags=None):
    return pltpu.CompilerParams(dimension_semantics=("arbitrary",) * n_axes,
                                vmem_limit_bytes=VMEM_LIMIT, flags=flags)


def _ffn_kernel(h_ref, pre_ref, wgu_ref, wd_ref, post_ref, o_ref):
    x = h_ref[...]
    xn = _rms(x, pre_ref[...]).astype(BF16)
    acc = None
    for c in range(FFN_DIM // FFN_CHUNK):
        lo = c * FFN_CHUNK
        g = _dot(xn, wgu_ref[:, lo:lo + FFN_CHUNK])
        u = _dot(xn, wgu_ref[:, FFN_DIM + lo:FFN_DIM + lo + FFN_CHUNK])
        a = (g * jax.nn.sigmoid(g) * u).astype(BF16)
        d = _dot(a, wd_ref[lo:lo + FFN_CHUNK, :])
        acc = d if acc is None else acc + d
    o_ref[...] = x + 0.5 * _rms(acc, post_ref[...])


def _ffn(h, layer, pre, wgu, wd, post, tm):
    n = h.shape[0]
    row = pl.BlockSpec((tm, D_MODEL), lambda i: (i, 0))
    return pl.pallas_call(
        _ffn_kernel,
        out_shape=jax.ShapeDtypeStruct(h.shape, F32),
        grid=(n // tm,),
        in_specs=[row,
                  _const_spec((None, 1, D_MODEL), (layer, 0, 0)),
                  _const_spec((None, D_MODEL, 2 * FFN_DIM), (layer, 0, 0)),
                  _const_spec((None, FFN_DIM, D_MODEL), (layer, 0, 0)),
                  _const_spec((None, 1, D_MODEL), (layer, 0, 0))],
        out_specs=row,
        compiler_params=_params(1),
        name="ffn",
    )(h, pre, wgu, wd, post)


def _inproj_kernel(h_ref, pre_ref, win_ref, qn_ref, wuq_ref, kvn_ref, wuk_ref, wuv_ref, te_ref,
                   cb_ref, ccu_ref, nq_ref, nk_ref, nv_ref, q_ref, k_ref, v_ref):
    tm = h_ref.shape[0]
    xn = _rms(h_ref[...], pre_ref[...]).astype(BF16)
    z = _dot(xn, win_ref[...])
    cb_ref[...] = z[:, 0:256]
    ccu_ref[...] = z[:, 256:512] * z[:, 512:768]
    nq_ref[...] = (z[:, 768:1024] * (NA_HEAD_DIM ** -0.5)).astype(BF16)
    nk_ref[...] = z[:, 1024:1280].astype(BF16)
    nv_ref[...] = z[:, 1280:1536].astype(BF16)
    ql = _rms(z[:, 1536:1536 + MLA_Q_RANK], qn_ref[...]).astype(BF16)
    qf = _dot(ql, wuq_ref[...])
    kvl = _rms(z[:, 2304:2304 + MLA_KV_RANK], kvn_ref[...]).astype(BF16)
    kn = _dot(kvl, wuk_ref[...])
    v_ref[...] = _dot(kvl, wuv_ref[...]).astype(BF16)
    te = te_ref[...]
    y = z[:, 2560:2560 + LANES] * te
    kr = y + pltpu.roll(y, MLA_ROPE_DIM, axis=1)
    low = lax.broadcasted_iota(jnp.int32, (tm, LANES), 1) < MLA_NOPE_DIM
    for h in range(MLA_HEADS):
        sl = slice(h * HEAD_SLOT, (h + 1) * HEAD_SLOT)
        qh = qf[:, sl]
        qt = qh * te
        q_ref[:, sl] = (jnp.where(low, qh, qt) if h % 2 == 0 else jnp.where(low, qt, qh)).astype(BF16)
    for j in range(MLA_HEADS // 2):
        kp = kn[:, j * LANES:(j + 1) * LANES]
        k_ref[:, (2 * j) * HEAD_SLOT:(2 * j + 1) * HEAD_SLOT] = jnp.where(low, kp, kr).astype(BF16)
        k_ref[:, (2 * j + 1) * HEAD_SLOT:(2 * j + 2) * HEAD_SLOT] = jnp.where(low, kr, kp).astype(BF16)


def _inproj(h, layer, w, te, te_blocks, tm):
    n = h.shape[0]
    row = lambda c: pl.BlockSpec((tm, c), lambda i: (i, 0))
    outs = [(256, F32), (256, F32), (256, BF16), (256, BF16), (256, BF16),
            (MLA_HEADS * HEAD_SLOT, BF16), (MLA_HEADS * HEAD_SLOT, BF16), (MLA_DIM, BF16)]
    return pl.pallas_call(
        _inproj_kernel,
        out_shape=[jax.ShapeDtypeStruct((n, c), dt) for c, dt in outs],
        grid=(n // tm,),
        in_specs=[row(D_MODEL),
                  _const_spec((None, 1, D_MODEL), (layer, 0, 0)),
                  _const_spec((None, D_MODEL, IN_COLS), (layer, 0, 0)),
                  _const_spec((None, 1, MLA_Q_RANK), (layer, 0, 0)),
                  _const_spec((None, MLA_Q_RANK, MLA_HEADS * HEAD_SLOT), (layer, 0, 0)),
                  _const_spec((None, 1, MLA_KV_RANK), (layer, 0, 0)),
                  _const_spec((None, MLA_KV_RANK, MLA_HEADS * MLA_NOPE_DIM), (layer, 0, 0)),
                  _const_spec((None, MLA_KV_RANK, MLA_DIM), (layer, 0, 0)),
                  pl.BlockSpec((tm, LANES), lambda i: (i % te_blocks, 0))],
        out_specs=[row(c) for c, _ in outs],
        compiler_params=_params(1),
        name="inproj",
    )(h, w["mix_pre"], w["w_in"], w["q_norm"], w["w_uq"], w["kv_norm"], w["w_uk"], w["w_uv"], te)


def _softmax_pv(s_list, v_list):
    m = s_list[0].max(axis=-1, keepdims=True)
    for s in s_list[1:]:
        m = jnp.maximum(m, s.max(axis=-1, keepdims=True))
    den = None
    out = None
    for s, v in zip(s_list, v_list):
        p = jnp.exp(s - m)
        l = p.sum(axis=-1, keepdims=True)
        o = _dot(p.astype(BF16), v)
        den = l if den is None else den + l
        out = o if out is None else out + o
    return out / den


def _na_kernel(vid_ref, st_ref, q_ref, k_ref, v_ref, km_ref, vm_ref, *rest):
    del vid_ref
    bias_refs, (o_ref, kcat, vcat) = rest[:NA_UNROLL], rest[NA_UNROLL:]
    i = pl.program_id(1)
    low = lax.broadcasted_iota(jnp.int32, (NA_QB, LANES), 1) < NA_HEAD_DIM
    zero_rows = jnp.zeros((NA_KCAT - NA_KB - N_META, NA_DIM), BF16)
    for u in range(NA_UNROLL):
        start = pl.multiple_of(st_ref[i * NA_UNROLL + u] * GRID_W, GRID_W)
        rows = slice(u * NA_QB, (u + 1) * NA_QB)
        for cat, win, meta in ((kcat, k_ref, km_ref), (vcat, v_ref, vm_ref)):
            cat[u, 0:NA_KB, :] = win[pl.ds(start, NA_KB), :]
            cat[u, NA_KB:NA_KB + N_META, :] = meta[...]
            cat[u, NA_KB + N_META:, :] = zero_rows
        for j in range(NA_HEADS // 2):
            sl = slice(j * LANES, (j + 1) * LANES)
            qp = q_ref[rows, sl]
            q2 = jnp.concatenate([jnp.where(low, qp, jnp.zeros_like(qp)),
                                  jnp.where(low, jnp.zeros_like(qp), qp)], axis=0)
            s = _dot_nt(q2, kcat[u, :, sl]) + bias_refs[u][j]
            p = jnp.exp(s - s.max(axis=-1, keepdims=True))
            o = _dot(p.astype(BF16), vcat[u, :, sl]) / p.sum(axis=-1, keepdims=True)
            o_ref[rows, sl] = jnp.where(low, o[:NA_QB], o[NA_QB:])


def _na(nq, nk, nv, nkm, nvm, bias, layer, vid, st, batch, seq, meta_off):
    nstep = seq // (NA_QB * NA_UNROLL)
    n_patterns = bias.shape[0] // DEPTH
    qspec = pl.BlockSpec((NA_QB * NA_UNROLL, NA_DIM), lambda b, i, vid, st: (b * nstep + i, 0))
    kvspec = pl.BlockSpec((seq, NA_DIM), lambda b, i, vid, st: (b, 0))
    mspec = pl.BlockSpec((N_META, NA_DIM), lambda b, i, vid, st: (meta_off + b, 0))
    bias_spec = lambda u: pl.BlockSpec(
        (None, NA_HEADS // 2, 2 * NA_QB, NA_KCAT),
        lambda b, i, vid, st: (layer * n_patterns + vid[i * NA_UNROLL + u], 0, 0, 0))
    grid_spec = pltpu.PrefetchScalarGridSpec(
        num_scalar_prefetch=2,
        grid=(batch, nstep),
        in_specs=[qspec, kvspec, kvspec, mspec, mspec] + [bias_spec(u) for u in range(NA_UNROLL)],
        out_specs=qspec,
        scratch_shapes=[pltpu.VMEM((NA_UNROLL, NA_KCAT, NA_DIM), BF16)] * 2)
    return pl.pallas_call(
        _na_kernel,
        out_shape=jax.ShapeDtypeStruct((batch * seq, NA_DIM), F32),
        grid_spec=grid_spec,
        compiler_params=_params(2),
        name="na",
    )(vid, st, nq, nk, nv, nkm, nvm, *([bias] * NA_UNROLL))


def _na_meta_kernel(q_ref, k_ref, v_ref, mb_ref, o_ref):
    low = lax.broadcasted_iota(jnp.int32, (N_META, LANES), 1) < NA_HEAD_DIM
    for j in range(NA_HEADS // 2):
        sl = slice(j * LANES, (j + 1) * LANES)
        qp = q_ref[:, sl]
        halves = []
        for half in range(2):
            h = 2 * j + half
            qh = jnp.where(low if half == 0 else jnp.logical_not(low), qp, jnp.zeros_like(qp))
            s = _dot_nt(qh, k_ref[:, sl]) + mb_ref[h:h + 1, :]
            halves.append(_softmax_pv([s], [v_ref[:, sl]]))
        o_ref[:, sl] = jnp.where(low, halves[0], halves[1])


def _na_meta(nqm, nkm, nvm, mb, layer):
    n = nqm.shape[0]
    spec = pl.BlockSpec((N_META, NA_DIM), lambda b: (b, 0))
    return pl.pallas_call(
        _na_meta_kernel,
        out_shape=jax.ShapeDtypeStruct((n, NA_DIM), F32),
        grid=(n // N_META,),
        in_specs=[spec, spec, spec, pl.BlockSpec((None, NA_HEADS, N_META), lambda b: (layer, 0, 0))],
        out_specs=spec,
        compiler_params=_params(1),
        name="na_meta",
    )(nqm, nkm, nvm, mb)


_MLA_EXP2_SCALE = (MLA_NOPE_DIM + MLA_ROPE_DIM) ** -0.5 * math.log2(math.e)


def _mla_kernel(q_ref, k_ref, v_ref, km_ref, vm_ref, o_ref, m_scr, l_scr, acc_scr, *, seq):
    tq = q_ref.shape[0]
    c = _MLA_EXP2_SCALE
    hsl = lambda h: slice(h * HEAD_SLOT, (h + 1) * HEAD_SLOT)
    vsl = lambda h: slice((h // 2) * LANES, (h // 2 + 1) * LANES)
    col = lax.broadcasted_iota(jnp.int32, (tq, LANES), 1)
    pad_mask = jnp.where(col < N_META, 0.0, MASK_VALUE).astype(F32)
    for h in range(MLA_HEADS):
        sm = _dot_nt(q_ref[:, hsl(h)], km_ref[:, hsl(h)]) + pad_mask
        m0 = sm.max(axis=-1, keepdims=True)
        p0 = jnp.exp2((sm - m0) * c)
        m_scr[h] = m0
        l_scr[h] = p0.sum(axis=-1, keepdims=True)
        acc_scr[h] = _dot(p0.astype(BF16), vm_ref[:, vsl(h)])

    def body(ci, carry):
        off = pl.multiple_of(ci * MLA_CK, MLA_CK)
        for h in range(MLA_HEADS):
            s = _dot_nt(q_ref[:, hsl(h)], k_ref[pl.ds(off, MLA_CK), hsl(h)])
            m = m_scr[h]
            mn = jnp.maximum(m, s.max(axis=-1, keepdims=True))
            alpha = jnp.exp2((m - mn) * c)
            p = jnp.exp2((s - mn) * c)
            m_scr[h] = mn
            l_scr[h] = alpha * l_scr[h] + p.sum(axis=-1, keepdims=True)
            acc_scr[h] = alpha * acc_scr[h] + _dot(p.astype(BF16), v_ref[pl.ds(off, MLA_CK), vsl(h)])
        return carry

    lax.fori_loop(0, seq // MLA_CK, body, 0)
    low = col < MLA_V_DIM
    for j in range(MLA_HEADS // 2):
        o_ref[:, vsl(2 * j)] = jnp.where(low, acc_scr[2 * j] / l_scr[2 * j], acc_scr[2 * j + 1] / l_scr[2 * j + 1])


def _mla(q, k, v, km, vm, batch, seq, meta_off, tq, q_rows_per_batch, q_off):
    nq = q_rows_per_batch // tq
    qspec = lambda c: pl.BlockSpec((tq, c), lambda b, i: (q_off + b * nq + i, 0))
    kvspec = lambda c: pl.BlockSpec((seq, c), lambda b, i: (b, 0))
    mspec = lambda c: pl.BlockSpec((LANES, c), lambda b, i: (meta_off + b, 0))
    ospec = pl.BlockSpec((tq, MLA_DIM), lambda b, i: (b * nq + i, 0))
    return pl.pallas_call(
        functools.partial(_mla_kernel, seq=seq),
        out_shape=jax.ShapeDtypeStruct((batch * q_rows_per_batch, MLA_DIM), F32),
        grid=(batch, nq),
        in_specs=[qspec(MLA_HEADS * HEAD_SLOT), kvspec(MLA_HEADS * HEAD_SLOT), kvspec(MLA_DIM),
                  mspec(MLA_HEADS * HEAD_SLOT), mspec(MLA_DIM)],
        out_specs=ospec,
        scratch_shapes=[pltpu.VMEM((MLA_HEADS, tq, 1), F32), pltpu.VMEM((MLA_HEADS, tq, 1), F32),
                        pltpu.VMEM((MLA_HEADS, tq, LANES), F32)],
        compiler_params=_params(2),
        name="mla",
    )(q, k, v, km, vm)


def _out_kernel(h_ref, cb_ref, ccu_ref, halo_a_ref, halo_b_ref, halo_c_ref, yna_ref, ymla_ref,
                convw_ref, cn_ref, nn_ref, mn_ref, wo_ref, post_ref, o_ref, *, meta, tiles_per_seq,
                prompt_batches):
    tm = h_ref.shape[0]
    i = pl.program_id(0)
    ccu = ccu_ref[...]
    if meta:
        prev_row = jnp.zeros((1, CONV_DIM), F32)
        next_row = jnp.where(i < prompt_batches, halo_a_ref[0:1, :], halo_b_ref[0:1, :])
    else:
        t = i % tiles_per_seq
        prev_row = jnp.where(t == 0, halo_c_ref[N_META - 1:N_META, :], halo_a_ref[7:8, :])
        next_row = jnp.where(t == tiles_per_seq - 1, jnp.zeros((1, CONV_DIM), F32), halo_b_ref[0:1, :])
    row = lax.broadcasted_iota(jnp.int32, (tm, CONV_DIM), 0)
    dn = jnp.where(row == 0, prev_row, pltpu.roll(ccu, 1, axis=0))
    up = jnp.where(row == tm - 1, next_row, pltpu.roll(ccu, tm - 1, axis=0))
    w = convw_ref[...]
    yc = cb_ref[...] * (w[0:1, :] * dn + w[1:2, :] * ccu + w[2:3, :] * up)
    y = jnp.concatenate([_rms(yc, cn_ref[...]), _rms(yna_ref[...], nn_ref[...]),
                         _rms(ymla_ref[...], mn_ref[...])], axis=-1).astype(BF16)
    o_ref[...] = h_ref[...] + _rms(_dot(y, wo_ref[...]), post_ref[...])


def _out(h, cb, ccu, halo_a, halo_b, halo_c, yna, ymla, layer, w, *, meta, tm, seq=None,
         meta_off=0, prompt_batches=0, seq_a=0, seq_b=0):
    n = h.shape[0]
    row = lambda c: pl.BlockSpec((tm, c), lambda i: (i, 0))
    if meta:
        tiles_per_seq = 1
        a_spec = pl.BlockSpec((8, CONV_DIM), lambda i: (jnp.minimum(i, prompt_batches - 1) * (seq_a // 8), 0))
        b_spec = pl.BlockSpec((8, CONV_DIM), lambda i: (jnp.maximum(i - prompt_batches, 0) * (seq_b // 8), 0))
        c_spec = pl.BlockSpec((8, CONV_DIM), lambda i: (0, 0))
    else:
        tiles_per_seq = seq // tm
        last = n // 8 - 1
        a_spec = pl.BlockSpec((8, CONV_DIM), lambda i: (jnp.maximum(i * (tm // 8) - 1, 0), 0))
        b_spec = pl.BlockSpec((8, CONV_DIM), lambda i: (jnp.minimum((i + 1) * (tm // 8), last), 0))
        c_spec = pl.BlockSpec((N_META, CONV_DIM), lambda i: (meta_off + i // tiles_per_seq, 0))
    return pl.pallas_call(
        functools.partial(_out_kernel, meta=meta, tiles_per_seq=tiles_per_seq, prompt_batches=prompt_batches),
        out_shape=jax.ShapeDtypeStruct(h.shape, F32),
        grid=(n // tm,),
        in_specs=[row(D_MODEL), row(CONV_DIM), row(CONV_DIM), a_spec, b_spec, c_spec,
                  row(NA_DIM), row(MLA_DIM),
                  _const_spec((None, 3, CONV_DIM), (layer, 0, 0)),
                  _const_spec((None, 1, CONV_DIM), (layer, 0, 0)),
                  _const_spec((None, 1, NA_DIM), (layer, 0, 0)),
                  _const_spec((None, 1, MLA_DIM), (layer, 0, 0)),
                  _const_spec((None, D_MODEL, D_MODEL), (layer, 0, 0)),
                  _const_spec((None, 1, D_MODEL), (layer, 0, 0))],
        out_specs=row(D_MODEL),
        compiler_params=_params(1),
        name="mix_out",
    )(h, cb, ccu, halo_a, halo_b, halo_c, yna, ymla,
      w["conv_w"], w["conv_on"], w["na_on"], w["mla_on"], w["w_o"], w["mix_post"])


def _na_patterns():
    patterns = []
    per_rows = {}
    for rows in (2048 // GRID_W, 4096 // GRID_W):
        win_h = min(NA_MAX_WIN_H, rows)
        vids, starts = [], []
        for blk in range(rows // NA_QROWS):
            r0 = blk * NA_QROWS
            start = int(np.clip(r0 - win_h // 2, 0, rows - NA_KROWS))
            qr = r0 + np.arange(NA_QROWS)[:, None]
            kr = start + np.arange(NA_KROWS)[None, :]
            rs = np.clip(qr - win_h // 2, 0, rows - win_h)
            inside = (kr >= rs) & (kr < rs + win_h)
            assert (inside.sum(axis=1) == win_h).all()
            rel = np.where(inside, kr - qr + NA_MAX_WIN_H - 1, -1)
            for n, p in enumerate(patterns):
                if np.array_equal(p, rel):
                    vids.append(n)
                    break
            else:
                vids.append(len(patterns))
                patterns.append(rel)
            starts.append(start)
        per_rows[rows] = (np.asarray(vids, np.int32), np.asarray(starts, np.int32))
    return np.stack(patterns), per_rows


_NA_REL_ROWS, _NA_BLOCKS = _na_patterns()


def _na_col_tables():
    qc = np.arange(GRID_W)[:, None]
    kc = np.arange(GRID_W)[None, :]
    cs = np.clip(qc - NA_WIN_W // 2, 0, GRID_W - NA_WIN_W)
    inside = (kc >= cs) & (kc < cs + NA_WIN_W)
    rel = kc - qc + NA_WIN_W - 1
    onehot = (rel[None] == np.arange(2 * NA_WIN_W - 1)[:, None, None]) & inside[None]
    return onehot.astype(np.float32), inside


_NA_COL_ONEHOT, _NA_COL_INSIDE = _na_col_tables()


def _na_bias_tables(rpb, meta_bias):
    blocks = jnp.einsum("lhrd,dqk->lhrqk", rpb.astype(F32), _NA_COL_ONEHOT, precision=lax.Precision.HIGHEST)
    blocks = jnp.where(_NA_COL_INSIDE, blocks, MASK_VALUE)
    masked = jnp.full(blocks.shape[:2] + (GRID_W, GRID_W), MASK_VALUE, F32)
    meta_cols = jnp.broadcast_to(meta_bias.astype(F32)[:, :, None, :], meta_bias.shape[:2] + (GRID_W, N_META))
    pad_cols = jnp.full(meta_bias.shape[:2] + (GRID_W, NA_KCAT - NA_KB - N_META), MASK_VALUE, F32)
    tables = []
    for rel in _NA_REL_ROWS:
        rows = [jnp.concatenate([masked if r < 0 else blocks[:, :, int(r)] for r in rel_q] + [meta_cols, pad_cols],
                                axis=-1) for rel_q in rel]
        tables.append(jnp.concatenate(rows, axis=-2))
    return jnp.stack(tables, axis=1).reshape(-1, NA_HEADS // 2, 2 * NA_QB, NA_KCAT)


def _rope_table(pos):
    half = MLA_ROPE_DIM // 2
    inv_freq = ROPE_THETA ** (-jnp.arange(half, dtype=F32) / half)
    ang = pos.astype(F32)[:, None] * inv_freq[None, :]
    cos = jnp.concatenate([jnp.cos(ang)] * 2, axis=-1)
    sin = jnp.concatenate([jnp.sin(ang)] * 2, axis=-1)
    return jnp.concatenate([cos, sin, cos, sin], axis=-1)


def _rot_cols(w):
    half = MLA_ROPE_DIM // 2
    return jnp.concatenate([-w[..., half:], w[..., :half]], axis=-1)


def _prep_weights(ffn1_pre, ffn1_gu, ffn1_down, ffn1_post, mix_pre, w_in, conv_w, na_rpb, na_mb, q_norm, w_uq,
                  kv_norm, w_ukv, conv_on, na_on, mla_on, w_o, mix_post, ffn2_pre, ffn2_gu, ffn2_down, ffn2_post):
    vec = lambda g: g[:, None, :]
    kpe = w_in[:, :, 2560:2560 + MLA_ROPE_DIM]
    kpe_rot = _rot_cols(kpe)
    w_in_ext = jnp.concatenate([w_in[:, :, :2560], kpe, kpe_rot, kpe, kpe_rot], axis=-1).astype(BF16)
    per_head = w_uq.reshape(DEPTH, MLA_Q_RANK, MLA_HEADS, MLA_NOPE_DIM + MLA_ROPE_DIM)
    nope, pe = per_head[..., :MLA_NOPE_DIM], per_head[..., MLA_NOPE_DIM:]
    even = jnp.concatenate([nope, pe, _rot_cols(pe)], axis=-1)
    odd = jnp.concatenate([pe, _rot_cols(pe), nope], axis=-1)
    is_even = (jnp.arange(MLA_HEADS) % 2 == 0)[None, None, :, None]
    w_uq_ext = jnp.where(is_even, even, odd).reshape(DEPTH, MLA_Q_RANK, MLA_HEADS * HEAD_SLOT).astype(BF16)
    kv_heads = w_ukv.reshape(DEPTH, MLA_KV_RANK, MLA_HEADS, MLA_NOPE_DIM + MLA_V_DIM)
    w_uk = kv_heads[..., :MLA_NOPE_DIM].reshape(DEPTH, MLA_KV_RANK, MLA_HEADS * MLA_NOPE_DIM).astype(BF16)
    w_uv = kv_heads[..., MLA_NOPE_DIM:].reshape(DEPTH, MLA_KV_RANK, MLA_DIM).astype(BF16)
    bias = _na_bias_tables(na_rpb, na_mb)
    return dict(
        ffn1=(vec(ffn1_pre), ffn1_gu.astype(BF16), ffn1_down.astype(BF16), vec(ffn1_post)),
        ffn2=(vec(ffn2_pre), ffn2_gu.astype(BF16), ffn2_down.astype(BF16), vec(ffn2_post)),
        mix_pre=vec(mix_pre), w_in=w_in_ext, q_norm=vec(q_norm), w_uq=w_uq_ext, kv_norm=vec(kv_norm),
        w_uk=w_uk, w_uv=w_uv, conv_w=conv_w, conv_on=vec(conv_on), na_on=vec(na_on), mla_on=vec(mla_on),
        w_o=w_o.astype(BF16), mix_post=vec(mix_post), na_bias=bias, na_mb=na_mb)


def _pad_meta_rows(x):
    c = x.shape[-1]
    x = x.reshape(-1, N_META, c)
    return jnp.pad(x, ((0, 0), (0, LANES - N_META), (0, 0))).reshape(-1, c)


def kernel(x_prompt, x_sample, meta_tokens, ffn1_pre_norm, ffn1_w_gu, ffn1_w_down, ffn1_post_norm, mix_pre_norm,
           w_in, conv_w, na_rpb, na_meta_bias, mla_q_norm, mla_w_uq, mla_kv_norm, mla_w_ukv, conv_out_norm,
           na_out_norm, mla_out_norm, w_o, mix_post_norm, ffn2_pre_norm, ffn2_w_gu, ffn2_w_down, ffn2_post_norm):
    w = _prep_weights(ffn1_pre_norm, ffn1_w_gu, ffn1_w_down, ffn1_post_norm, mix_pre_norm, w_in, conv_w, na_rpb,
                      na_meta_bias, mla_q_norm, mla_w_uq, mla_kv_norm, mla_w_ukv, conv_out_norm, na_out_norm,
                      mla_out_norm, w_o, mix_post_norm, ffn2_pre_norm, ffn2_w_gu, ffn2_w_down, ffn2_post_norm)
    groups = []
    for x in (x_prompt, x_sample):
        b, t, _ = x.shape
        groups.append(dict(batch=b, seq=t, te=_rope_table(N_META + jnp.arange(t))))
    nb = [g["batch"] for g in groups]
    meta_off = [0, nb[0]]
    n_meta_rows = sum(nb) * N_META
    te_meta = jnp.tile(_rope_table(jnp.arange(N_META)), (sum(nb), 1))
    hs = [x_prompt.reshape(-1, D_MODEL), x_sample.reshape(-1, D_MODEL),
          jnp.tile(meta_tokens.astype(F32), (sum(nb), 1))]
    tms = [ROW_TILE, ROW_TILE, n_meta_rows]

    for layer in range(DEPTH):
        hs = [_ffn(h, layer, *w["ffn1"], tm) for h, tm in zip(hs, tms)]
        proj = [_inproj(hs[g], layer, w, groups[g]["te"], groups[g]["seq"] // ROW_TILE, ROW_TILE) for g in range(2)]
        proj.append(_inproj(hs[2], layer, w, te_meta, 1, n_meta_rows))
        cb_m, ccu_m, nq_m, nk_m, nv_m, q_m, k_m, v_m = proj[2]
        yna_m = _na_meta(nq_m, nk_m, nv_m, w["na_mb"], layer)
        k_mp, v_mp = _pad_meta_rows(k_m), _pad_meta_rows(v_m)
        new_hs, ymla_m = [], []
        for g in range(2):
            cb, ccu, nq, nk, nv, q, k, v = proj[g]
            b, t = groups[g]["batch"], groups[g]["seq"]
            vid, st = _NA_BLOCKS[t // GRID_W]
            yna = _na(nq, nk, nv, nk_m, nv_m, w["na_bias"], layer,
                      jnp.asarray(vid), jnp.asarray(st), b, t, meta_off[g])
            ymla = _mla(q, k, v, k_mp, v_mp, b, t, meta_off[g], MLA_TQ, t, 0)
            ymla_m.append(_mla(q_m, k, v, k_mp, v_mp, b, t, meta_off[g], N_META, N_META, meta_off[g]))
            new_hs.append(_out(hs[g], cb, ccu, ccu, ccu, ccu_m, yna, ymla, layer, w, meta=False, tm=ROW_TILE,
                               seq=t, meta_off=meta_off[g]))
        new_hs.append(_out(hs[2], cb_m, ccu_m, proj[0][1], proj[1][1], ccu_m, yna_m, jnp.concatenate(ymla_m, axis=0),
                           layer, w, meta=True, tm=N_META, prompt_batches=nb[0],
                           seq_a=groups[0]["seq"], seq_b=groups[1]["seq"]))
        hs = new_hs
        last = layer == DEPTH - 1
        hs = [h if (last and n == 2) else _ffn(h, layer, *w["ffn2"], tm) for n, (h, tm) in enumerate(zip(hs, tms))]
    return (hs[0].reshape(x_prompt.shape), hs[1].reshape(x_sample.shape))
```

```python
import functools
import math

import numpy as np
import jax
import jax.numpy as jnp
from jax import lax
from jax.experimental import pallas as pl
from jax.experimental.pallas import tpu as pltpu

F32 = jnp.float32
BF16 = jnp.bfloat16

D_MODEL = 1024
DEPTH = 4
N_META = 16
GRID_W = 64
CONV_DIM = 256
NA_HEADS = 4
NA_HEAD_DIM = 64
NA_DIM = NA_HEADS * NA_HEAD_DIM
NA_MAX_WIN_H = 8
NA_WIN_W = 16
MLA_HEADS = 8
MLA_NOPE_DIM = 64
MLA_ROPE_DIM = 32
MLA_V_DIM = 64
MLA_Q_RANK = 768
MLA_KV_RANK = 256
MLA_DIM = MLA_HEADS * MLA_V_DIM
FFN_DIM = 2816
ROPE_THETA = 10000.0
RMS_EPS = 1e-6

LANES = 128
HEAD_SLOT = 128
IN_COLS = 6 * 256 + MLA_Q_RANK + MLA_KV_RANK + LANES
FFN_CHUNK = 256
ROW_TILE = 512
HALO_ROWS = 16
NA_QROWS = 2
NA_KROWS = NA_MAX_WIN_H + NA_QROWS - 1
NA_QB = NA_QROWS * GRID_W
NA_KB = NA_KROWS * GRID_W
NA_KCAT = 640
NA_UNROLL = 2
MLA_TQ = 256
MLA_CK = 1024
MASK_VALUE = -1e30
VMEM_LIMIT = 56 * 1024 * 1024


def _rms(x, g):
    ms = jnp.mean(x * x, axis=-1, keepdims=True)
    return x * lax.rsqrt(ms + RMS_EPS) * g


def _dot(a, b):
    return jnp.dot(a, b, preferred_element_type=F32)


def _dot_nt(a, b):
    return lax.dot_general(a, b, (((1,), (1,)), ((), ())), preferred_element_type=F32)


def _const_spec(block_shape, index):
    return pl.BlockSpec(block_shape, lambda *_: index, pipeline_mode=pl.Buffered(1))


def _params(n_axes, flags=None):
    return pltpu.CompilerParams(dimension_semantics=("arbitrary",) * n_axes,
                                vmem_limit_bytes=VMEM_LIMIT, flags=flags)


def _ffn_kernel(h_ref, pre_ref, wgu_ref, wd_ref, post_ref, o_ref):
    x = h_ref[...]
    xn = _rms(x, pre_ref[...]).astype(BF16)
    acc = None
    for c in range(FFN_DIM // FFN_CHUNK):
        lo = c * FFN_CHUNK
        g = _dot(xn, wgu_ref[:, lo:lo + FFN_CHUNK])
        u = _dot(xn, wgu_ref[:, FFN_DIM + lo:FFN_DIM + lo + FFN_CHUNK])
        a = (g * jax.nn.sigmoid(g) * u).astype(BF16)
        d = _dot(a, wd_ref[lo:lo + FFN_CHUNK, :])
        acc = d if acc is None else acc + d
    o_ref[...] = x + 0.5 * _rms(acc, post_ref[...])


def _ffn(h, layer, pre, wgu, wd, post, tm):
    n = h.shape[0]
    row = pl.BlockSpec((tm, D_MODEL), lambda i: (i, 0))
    return pl.pallas_call(
        _ffn_kernel,
        out_shape=jax.ShapeDtypeStruct(h.shape, F32),
        grid=(n // tm,),
        in_specs=[row,
                  _const_spec((None, 1, D_MODEL), (layer, 0, 0)),
                  _const_spec((None, D_MODEL, 2 * FFN_DIM), (layer, 0, 0)),
                  _const_spec((None, FFN_DIM, D_MODEL), (layer, 0, 0)),
                  _const_spec((None, 1, D_MODEL), (layer, 0, 0))],
        out_specs=row,
        compiler_params=_params(1),
        name="ffn",
    )(h, pre, wgu, wd, post)


def _inproj_kernel(h_ref, pre_ref, win_ref, qn_ref, wuq_ref, kvn_ref, wuk_ref, wuv_ref, te_ref,
                   cb_ref, ccu_ref, nq_ref, nk_ref, nv_ref, q_ref, k_ref, v_ref):
    tm = h_ref.shape[0]
    xn = _rms(h_ref[...], pre_ref[...]).astype(BF16)
    z = _dot(xn, win_ref[...])
    cb_ref[...] = z[:, 0:256].astype(BF16)
    ccu_ref[...] = (z[:, 256:512] * z[:, 512:768]).astype(BF16)
    nq_ref[...] = (z[:, 768:1024] * (NA_HEAD_DIM ** -0.5)).astype(BF16)
    nk_ref[...] = z[:, 1024:1280].astype(BF16)
    nv_ref[...] = z[:, 1280:1536].astype(BF16)
    ql = _rms(z[:, 1536:1536 + MLA_Q_RANK], qn_ref[...]).astype(BF16)
    qf = _dot(ql, wuq_ref[...])
    kvl = _rms(z[:, 2304:2304 + MLA_KV_RANK], kvn_ref[...]).astype(BF16)
    kn = _dot(kvl, wuk_ref[...])
    v_ref[...] = _dot(kvl, wuv_ref[...]).astype(BF16)
    te = te_ref[...]
    y = z[:, 2560:2560 + LANES] * te
    kr = y + pltpu.roll(y, MLA_ROPE_DIM, axis=1)
    low = lax.broadcasted_iota(jnp.int32, (tm, LANES), 1) < MLA_NOPE_DIM
    for h in range(MLA_HEADS):
        sl = slice(h * HEAD_SLOT, (h + 1) * HEAD_SLOT)
        qh = qf[:, sl] * _MLA_EXP2_SCALE
        qt = qh * te
        q_ref[:, sl] = (jnp.where(low, qh, qt) if h % 2 == 0 else jnp.where(low, qt, qh)).astype(BF16)
    for j in range(MLA_HEADS // 2):
        kp = kn[:, j * LANES:(j + 1) * LANES]
        k_ref[:, (2 * j) * HEAD_SLOT:(2 * j + 1) * HEAD_SLOT] = jnp.where(low, kp, kr).astype(BF16)
        k_ref[:, (2 * j + 1) * HEAD_SLOT:(2 * j + 2) * HEAD_SLOT] = jnp.where(low, kr, kp).astype(BF16)


def _inproj(h, layer, w, te, te_blocks, tm):
    n = h.shape[0]
    row = lambda c: pl.BlockSpec((tm, c), lambda i: (i, 0))
    outs = [(256, BF16), (256, BF16), (256, BF16), (256, BF16), (256, BF16),
            (MLA_HEADS * HEAD_SLOT, BF16), (MLA_HEADS * HEAD_SLOT, BF16), (MLA_DIM, BF16)]
    return pl.pallas_call(
        _inproj_kernel,
        out_shape=[jax.ShapeDtypeStruct((n, c), dt) for c, dt in outs],
        grid=(n // tm,),
        in_specs=[row(D_MODEL),
                  _const_spec((None, 1, D_MODEL), (layer, 0, 0)),
                  _const_spec((None, D_MODEL, IN_COLS), (layer, 0, 0)),
                  _const_spec((None, 1, MLA_Q_RANK), (layer, 0, 0)),
                  _const_spec((None, MLA_Q_RANK, MLA_HEADS * HEAD_SLOT), (layer, 0, 0)),
                  _const_spec((None, 1, MLA_KV_RANK), (layer, 0, 0)),
                  _const_spec((None, MLA_KV_RANK, MLA_HEADS * MLA_NOPE_DIM), (layer, 0, 0)),
                  _const_spec((None, MLA_KV_RANK, MLA_DIM), (layer, 0, 0)),
                  pl.BlockSpec((tm, LANES), lambda i: (i % te_blocks, 0))],
        out_specs=[row(c) for c, _ in outs],
        compiler_params=_params(1),
        name="inproj",
    )(h, w["mix_pre"], w["w_in"], w["q_norm"], w["w_uq"], w["kv_norm"], w["w_uk"], w["w_uv"], te)


def _softmax_pv(s_list, v_list):
    m = s_list[0].max(axis=-1, keepdims=True)
    for s in s_list[1:]:
        m = jnp.maximum(m, s.max(axis=-1, keepdims=True))
    den = None
    out = None
    for s, v in zip(s_list, v_list):
        p = jnp.exp(s - m)
        l = p.sum(axis=-1, keepdims=True)
        o = _dot(p.astype(BF16), v)
        den = l if den is None else den + l
        out = o if out is None else out + o
    return out / den


def _na_kernel(vid_ref, st_ref, q_ref, k_ref, v_ref, km_ref, vm_ref, *rest):
    del vid_ref
    bias_refs, (o_ref, kcat, vcat) = rest[:NA_UNROLL], rest[NA_UNROLL:]
    i = pl.program_id(1)
    low = lax.broadcasted_iota(jnp.int32, (NA_QB, LANES), 1) < NA_HEAD_DIM
    zero_rows = jnp.zeros((NA_KCAT - NA_KB - N_META, NA_DIM), BF16)
    for u in range(NA_UNROLL):
        start = pl.multiple_of(st_ref[i * NA_UNROLL + u] * GRID_W, GRID_W)
        for cat, win, meta in ((kcat, k_ref, km_ref), (vcat, v_ref, vm_ref)):
            cat[u, 0:NA_KB, :] = win[pl.ds(start, NA_KB), :]
            cat[u, NA_KB:NA_KB + N_META, :] = meta[...]
            cat[u, NA_KB + N_META:, :] = zero_rows
    chains = [(u, j) for u in range(NA_UNROLL) for j in range(NA_HEADS // 2)]
    sl = lambda j: slice(j * LANES, (j + 1) * LANES)
    rows = lambda u: slice(u * NA_QB, (u + 1) * NA_QB)

    def scores(u, j):
        qp = q_ref[rows(u), sl(j)]
        q2 = jnp.concatenate([jnp.where(low, qp, jnp.zeros_like(qp)),
                              jnp.where(low, jnp.zeros_like(qp), qp)], axis=0)
        return _dot_nt(q2, kcat[u, :, sl(j)]) + bias_refs[u][j]

    s_next = scores(*chains[0])
    for n, (u, j) in enumerate(chains):
        s = s_next
        if n + 1 < len(chains):
            s_next = scores(*chains[n + 1])
        p = jnp.exp(s - s.max(axis=-1, keepdims=True))
        o = _dot(p.astype(BF16), vcat[u, :, sl(j)]) / p.sum(axis=-1, keepdims=True)
        o_ref[rows(u), sl(j)] = jnp.where(low, o[:NA_QB], o[NA_QB:]).astype(BF16)


def _na(nq, nk, nv, nkm, nvm, bias, layer, vid, st, batch, seq, meta_off):
    nstep = seq // (NA_QB * NA_UNROLL)
    n_patterns = bias.shape[0] // DEPTH
    qspec = pl.BlockSpec((NA_QB * NA_UNROLL, NA_DIM), lambda b, i, vid, st: (b * nstep + i, 0))
    kvspec = pl.BlockSpec((seq, NA_DIM), lambda b, i, vid, st: (b, 0))
    mspec = pl.BlockSpec((N_META, NA_DIM), lambda b, i, vid, st: (meta_off + b, 0))
    bias_spec = lambda u: pl.BlockSpec(
        (None, NA_HEADS // 2, 2 * NA_QB, NA_KCAT),
        lambda b, i, vid, st: (layer * n_patterns + vid[i * NA_UNROLL + u], 0, 0, 0))
    grid_spec = pltpu.PrefetchScalarGridSpec(
        num_scalar_prefetch=2,
        grid=(batch, nstep),
        in_specs=[qspec, kvspec, kvspec, mspec, mspec] + [bias_spec(u) for u in range(NA_UNROLL)],
        out_specs=qspec,
        scratch_shapes=[pltpu.VMEM((NA_UNROLL, NA_KCAT, NA_DIM), BF16)] * 2)
    return pl.pallas_call(
        _na_kernel,
        out_shape=jax.ShapeDtypeStruct((batch * seq, NA_DIM), BF16),
        grid_spec=grid_spec,
        compiler_params=_params(2),
        name="na",
    )(vid, st, nq, nk, nv, nkm, nvm, *([bias] * NA_UNROLL))


def _na_meta_kernel(q_ref, k_ref, v_ref, mb_ref, o_ref):
    low = lax.broadcasted_iota(jnp.int32, (N_META, LANES), 1) < NA_HEAD_DIM
    for j in range(NA_HEADS // 2):
        sl = slice(j * LANES, (j + 1) * LANES)
        qp = q_ref[:, sl]
        halves = []
        for half in range(2):
            h = 2 * j + half
            qh = jnp.where(low if half == 0 else jnp.logical_not(low), qp, jnp.zeros_like(qp))
            s = _dot_nt(qh, k_ref[:, sl]) + mb_ref[h:h + 1, :]
            halves.append(_softmax_pv([s], [v_ref[:, sl]]))
        o_ref[:, sl] = jnp.where(low, halves[0], halves[1]).astype(BF16)


def _na_meta(nqm, nkm, nvm, mb, layer):
    n = nqm.shape[0]
    spec = pl.BlockSpec((N_META, NA_DIM), lambda b: (b, 0))
    return pl.pallas_call(
        _na_meta_kernel,
        out_shape=jax.ShapeDtypeStruct((n, NA_DIM), BF16),
        grid=(n // N_META,),
        in_specs=[spec, spec, spec, pl.BlockSpec((None, NA_HEADS, N_META), lambda b: (layer, 0, 0))],
        out_specs=spec,
        compiler_params=_params(1),
        name="na_meta",
    )(nqm, nkm, nvm, mb)


_MLA_EXP2_SCALE = (MLA_NOPE_DIM + MLA_ROPE_DIM) ** -0.5 * math.log2(math.e)


def _mla_kernel(q_ref, k_ref, v_ref, km_ref, vm_ref, o_ref, m_scr, l_scr, acc_scr, *, seq):
    tq = q_ref.shape[0]
    hsl = lambda h: slice(h * HEAD_SLOT, (h + 1) * HEAD_SLOT)
    vsl = lambda h: slice((h // 2) * LANES, (h // 2 + 1) * LANES)
    tiles = lambda x: [x[:, t * LANES:(t + 1) * LANES] for t in range(x.shape[1] // LANES)]

    def softmax_piece(s, m_old):
        parts = tiles(s)
        lane_max = functools.reduce(jnp.maximum, parts)
        mn = jnp.broadcast_to(lane_max.max(axis=-1, keepdims=True), (tq, LANES))
        if m_old is not None:
            mn = jnp.maximum(m_old, mn)
        ps = [jnp.exp2(part - mn) for part in parts]
        return mn, jnp.concatenate([p.astype(BF16) for p in ps], axis=-1), functools.reduce(jnp.add, ps)

    col = lax.broadcasted_iota(jnp.int32, (tq, LANES), 1)
    pad_mask = jnp.where(col < N_META, 0.0, MASK_VALUE).astype(F32)

    def chunk(off, first):
        def scores(h):
            s = _dot_nt(q_ref[:, hsl(h)], k_ref[pl.ds(off, MLA_CK), hsl(h)])
            if first:
                s = jnp.concatenate([s, _dot_nt(q_ref[:, hsl(h)], km_ref[:, hsl(h)]) + pad_mask], axis=-1)
            return s

        def values(p, h):
            o = _dot(p[:, :MLA_CK], v_ref[pl.ds(off, MLA_CK), vsl(h)])
            return o + _dot(p[:, MLA_CK:], vm_ref[:, vsl(h)]) if first else o

        s_next = scores(0)
        for h in range(MLA_HEADS):
            s = s_next
            if h + 1 < MLA_HEADS:
                s_next = scores(h + 1)
            if first:
                m_scr[h], p, l_scr[h] = softmax_piece(s, None)
                acc_scr[h] = values(p, h)
            else:
                m = m_scr[h]
                mn, p, lsum = softmax_piece(s, m)
                alpha = jnp.exp2(m - mn)
                m_scr[h] = mn
                l_scr[h] = alpha * l_scr[h] + lsum
                acc_scr[h] = alpha * acc_scr[h] + values(p, h)

    for ci in range(seq // MLA_CK):
        chunk(ci * MLA_CK, ci == 0)
    low = col < MLA_V_DIM
    out = [acc_scr[h] / l_scr[h].sum(axis=-1, keepdims=True) for h in range(MLA_HEADS)]
    for j in range(MLA_HEADS // 2):
        o_ref[:, vsl(2 * j)] = jnp.where(low, out[2 * j], out[2 * j + 1]).astype(BF16)


def _mla(q, k, v, km, vm, batch, seq, meta_off, tq, q_rows_per_batch, q_off):
    nq = q_rows_per_batch // tq
    qspec = lambda c: pl.BlockSpec((tq, c), lambda b, i: (q_off + b * nq + i, 0))
    kvspec = lambda c: pl.BlockSpec((seq, c), lambda b, i: (b, 0))
    mspec = lambda c: pl.BlockSpec((LANES, c), lambda b, i: (meta_off + b, 0))
    ospec = pl.BlockSpec((tq, MLA_DIM), lambda b, i: (b * nq + i, 0))
    return pl.pallas_call(
        functools.partial(_mla_kernel, seq=seq),
        out_shape=jax.ShapeDtypeStruct((batch * q_rows_per_batch, MLA_DIM), BF16),
        grid=(batch, nq),
        in_specs=[qspec(MLA_HEADS * HEAD_SLOT), kvspec(MLA_HEADS * HEAD_SLOT), kvspec(MLA_DIM),
                  mspec(MLA_HEADS * HEAD_SLOT), mspec(MLA_DIM)],
        out_specs=ospec,
        scratch_shapes=[pltpu.VMEM((MLA_HEADS, tq, LANES), F32)] * 3,
        compiler_params=_params(2),
        name="mla",
    )(q, k, v, km, vm)


def _out_kernel(h_ref, cb_ref, ccu_ref, halo_a_ref, halo_b_ref, halo_c_ref, yna_ref, ymla_ref,
                convw_ref, cn_ref, nn_ref, mn_ref, wo_ref, post_ref, o_ref, *, meta, tiles_per_seq,
                prompt_batches):
    tm = h_ref.shape[0]
    i = pl.program_id(0)
    ccu = ccu_ref[...].astype(F32)
    first_row = lambda ref: ref[...].astype(F32)[0:1, :]
    last_row = lambda ref: ref[...].astype(F32)[HALO_ROWS - 1:HALO_ROWS, :]
    if meta:
        prev_row = jnp.zeros((1, CONV_DIM), F32)
        next_row = jnp.where(i < prompt_batches, first_row(halo_a_ref), first_row(halo_b_ref))
    else:
        t = i % tiles_per_seq
        prev_row = jnp.where(t == 0, last_row(halo_c_ref), last_row(halo_a_ref))
        next_row = jnp.where(t == tiles_per_seq - 1, jnp.zeros((1, CONV_DIM), F32), first_row(halo_b_ref))
    row = lax.broadcasted_iota(jnp.int32, (tm, CONV_DIM), 0)
    dn = jnp.where(row == 0, prev_row, pltpu.roll(ccu, 1, axis=0))
    up = jnp.where(row == tm - 1, next_row, pltpu.roll(ccu, tm - 1, axis=0))
    w = convw_ref[...]
    yc = cb_ref[...].astype(F32) * (w[0:1, :] * dn + w[1:2, :] * ccu + w[2:3, :] * up)
    y = jnp.concatenate([_rms(yc, cn_ref[...]), _rms(yna_ref[...].astype(F32), nn_ref[...]),
                         _rms(ymla_ref[...].astype(F32), mn_ref[...])], axis=-1).astype(BF16)
    o_ref[...] = h_ref[...] + _rms(_dot(y, wo_ref[...]), post_ref[...])


def _out(h, cb, ccu, halo_a, halo_b, halo_c, yna, ymla, layer, w, *, meta, tm, seq=None,
         meta_off=0, prompt_batches=0, seq_a=0, seq_b=0):
    n = h.shape[0]
    row = lambda c: pl.BlockSpec((tm, c), lambda i: (i, 0))
    halo = lambda index_map: pl.BlockSpec((HALO_ROWS, CONV_DIM), index_map)
    if meta:
        tiles_per_seq = 1
        a_spec = halo(lambda i: (jnp.minimum(i, prompt_batches - 1) * (seq_a // HALO_ROWS), 0))
        b_spec = halo(lambda i: (jnp.maximum(i - prompt_batches, 0) * (seq_b // HALO_ROWS), 0))
        c_spec = halo(lambda i: (0, 0))
    else:
        tiles_per_seq = seq // tm
        last = n // HALO_ROWS - 1
        a_spec = halo(lambda i: (jnp.maximum(i * (tm // HALO_ROWS) - 1, 0), 0))
        b_spec = halo(lambda i: (jnp.minimum((i + 1) * (tm // HALO_ROWS), last), 0))
        c_spec = halo(lambda i: (meta_off + i // tiles_per_seq, 0))
    return pl.pallas_call(
        functools.partial(_out_kernel, meta=meta, tiles_per_seq=tiles_per_seq, prompt_batches=prompt_batches),
        out_shape=jax.ShapeDtypeStruct(h.shape, F32),
        grid=(n // tm,),
        in_specs=[row(D_MODEL), row(CONV_DIM), row(CONV_DIM), a_spec, b_spec, c_spec,
                  row(NA_DIM), row(MLA_DIM),
                  _const_spec((None, 3, CONV_DIM), (layer, 0, 0)),
                  _const_spec((None, 1, CONV_DIM), (layer, 0, 0)),
                  _const_spec((None, 1, NA_DIM), (layer, 0, 0)),
                  _const_spec((None, 1, MLA_DIM), (layer, 0, 0)),
                  _const_spec((None, D_MODEL, D_MODEL), (layer, 0, 0)),
                  _const_spec((None, 1, D_MODEL), (layer, 0, 0))],
        out_specs=row(D_MODEL),
        compiler_params=_params(1),
        name="mix_out",
    )(h, cb, ccu, halo_a, halo_b, halo_c, yna, ymla,
      w["conv_w"], w["conv_on"], w["na_on"], w["mla_on"], w["w_o"], w["mix_post"])


def _na_patterns():
    patterns = []
    per_rows = {}
    for rows in (2048 // GRID_W, 4096 // GRID_W):
        win_h = min(NA_MAX_WIN_H, rows)
        vids, starts = [], []
        for blk in range(rows // NA_QROWS):
            r0 = blk * NA_QROWS
            start = int(np.clip(r0 - win_h // 2, 0, rows - NA_KROWS))
            qr = r0 + np.arange(NA_QROWS)[:, None]
            kr = start + np.arange(NA_KROWS)[None, :]
            rs = np.clip(qr - win_h // 2, 0, rows - win_h)
            inside = (kr >= rs) & (kr < rs + win_h)
            assert (inside.sum(axis=1) == win_h).all()
            rel = np.where(inside, kr - qr + NA_MAX_WIN_H - 1, -1)
            for n, p in enumerate(patterns):
                if np.array_equal(p, rel):
                    vids.append(n)
                    break
            else:
                vids.append(len(patterns))
                patterns.append(rel)
            starts.append(start)
        per_rows[rows] = (np.asarray(vids, np.int32), np.asarray(starts, np.int32))
    return np.stack(patterns), per_rows


_NA_REL_ROWS, _NA_BLOCKS = _na_patterns()


def _na_col_tables():
    qc = np.arange(GRID_W)[:, None]
    kc = np.arange(GRID_W)[None, :]
    cs = np.clip(qc - NA_WIN_W // 2, 0, GRID_W - NA_WIN_W)
    inside = (kc >= cs) & (kc < cs + NA_WIN_W)
    rel = kc - qc + NA_WIN_W - 1
    onehot = (rel[None] == np.arange(2 * NA_WIN_W - 1)[:, None, None]) & inside[None]
    return onehot.astype(np.float32), inside


_NA_COL_ONEHOT, _NA_COL_INSIDE = _na_col_tables()


def _na_bias_tables(rpb, meta_bias):
    blocks = jnp.einsum("lhrd,dqk->lhrqk", rpb.astype(F32), _NA_COL_ONEHOT, precision=lax.Precision.HIGHEST)
    blocks = jnp.where(_NA_COL_INSIDE, blocks, MASK_VALUE)
    masked = jnp.full(blocks.shape[:2] + (GRID_W, GRID_W), MASK_VALUE, F32)
    meta_cols = jnp.broadcast_to(meta_bias.astype(F32)[:, :, None, :], meta_bias.shape[:2] + (GRID_W, N_META))
    pad_cols = jnp.full(meta_bias.shape[:2] + (GRID_W, NA_KCAT - NA_KB - N_META), MASK_VALUE, F32)
    tables = []
    for rel in _NA_REL_ROWS:
        rows = [jnp.concatenate([masked if r < 0 else blocks[:, :, int(r)] for r in rel_q] + [meta_cols, pad_cols],
                                axis=-1) for rel_q in rel]
        tables.append(jnp.concatenate(rows, axis=-2))
    return jnp.stack(tables, axis=1).reshape(-1, NA_HEADS // 2, 2 * NA_QB, NA_KCAT)


def _rope_table(pos):
    half = MLA_ROPE_DIM // 2
    inv_freq = ROPE_THETA ** (-jnp.arange(half, dtype=F32) / half)
    ang = pos.astype(F32)[:, None] * inv_freq[None, :]
    cos = jnp.concatenate([jnp.cos(ang)] * 2, axis=-1)
    sin = jnp.concatenate([jnp.sin(ang)] * 2, axis=-1)
    return jnp.concatenate([cos, sin, cos, sin], axis=-1)


def _rot_cols(w):
    half = MLA_ROPE_DIM // 2
    return jnp.concatenate([-w[..., half:], w[..., :half]], axis=-1)


def _prep_weights(ffn1_pre, ffn1_gu, ffn1_down, ffn1_post, mix_pre, w_in, conv_w, na_rpb, na_mb, q_norm, w_uq,
                  kv_norm, w_ukv, conv_on, na_on, mla_on, w_o, mix_post, ffn2_pre, ffn2_gu, ffn2_down, ffn2_post):
    vec = lambda g: g[:, None, :]
    kpe = w_in[:, :, 2560:2560 + MLA_ROPE_DIM]
    kpe_rot = _rot_cols(kpe)
    w_in_ext = jnp.concatenate([w_in[:, :, :2560], kpe, kpe_rot, kpe, kpe_rot], axis=-1).astype(BF16)
    per_head = w_uq.reshape(DEPTH, MLA_Q_RANK, MLA_HEADS, MLA_NOPE_DIM + MLA_ROPE_DIM)
    nope, pe = per_head[..., :MLA_NOPE_DIM], per_head[..., MLA_NOPE_DIM:]
    even = jnp.concatenate([nope, pe, _rot_cols(pe)], axis=-1)
    odd = jnp.concatenate([pe, _rot_cols(pe), nope], axis=-1)
    is_even = (jnp.arange(MLA_HEADS) % 2 == 0)[None, None, :, None]
    w_uq_ext = jnp.where(is_even, even, odd).reshape(DEPTH, MLA_Q_RANK, MLA_HEADS * HEAD_SLOT).astype(BF16)
    kv_heads = w_ukv.reshape(DEPTH, MLA_KV_RANK, MLA_HEADS, MLA_NOPE_DIM + MLA_V_DIM)
    w_uk = kv_heads[..., :MLA_NOPE_DIM].reshape(DEPTH, MLA_KV_RANK, MLA_HEADS * MLA_NOPE_DIM).astype(BF16)
    w_uv = kv_heads[..., MLA_NOPE_DIM:].reshape(DEPTH, MLA_KV_RANK, MLA_DIM).astype(BF16)
    bias = _na_bias_tables(na_rpb, na_mb)
    return dict(
        ffn1=(vec(ffn1_pre), ffn1_gu.astype(BF16), ffn1_down.astype(BF16), vec(ffn1_post)),
        ffn2=(vec(ffn2_pre), ffn2_gu.astype(BF16), ffn2_down.astype(BF16), vec(ffn2_post)),
        mix_pre=vec(mix_pre), w_in=w_in_ext, q_norm=vec(q_norm), w_uq=w_uq_ext, kv_norm=vec(kv_norm),
        w_uk=w_uk, w_uv=w_uv, conv_w=conv_w, conv_on=vec(conv_on), na_on=vec(na_on), mla_on=vec(mla_on),
        w_o=w_o.astype(BF16), mix_post=vec(mix_post), na_bias=bias, na_mb=na_mb)


def _pad_meta_rows(x):
    c = x.shape[-1]
    x = x.reshape(-1, N_META, c)
    return jnp.pad(x, ((0, 0), (0, LANES - N_META), (0, 0))).reshape(-1, c)


def kernel(x_prompt, x_sample, meta_tokens, ffn1_pre_norm, ffn1_w_gu, ffn1_w_down, ffn1_post_norm, mix_pre_norm,
           w_in, conv_w, na_rpb, na_meta_bias, mla_q_norm, mla_w_uq, mla_kv_norm, mla_w_ukv, conv_out_norm,
           na_out_norm, mla_out_norm, w_o, mix_post_norm, ffn2_pre_norm, ffn2_w_gu, ffn2_w_down, ffn2_post_norm):
    w = _prep_weights(ffn1_pre_norm, ffn1_w_gu, ffn1_w_down, ffn1_post_norm, mix_pre_norm, w_in, conv_w, na_rpb,
                      na_meta_bias, mla_q_norm, mla_w_uq, mla_kv_norm, mla_w_ukv, conv_out_norm, na_out_norm,
                      mla_out_norm, w_o, mix_post_norm, ffn2_pre_norm, ffn2_w_gu, ffn2_w_down, ffn2_post_norm)
    groups = []
    for x in (x_prompt, x_sample):
        b, t, _ = x.shape
        groups.append(dict(batch=b, seq=t, te=_rope_table(N_META + jnp.arange(t))))
    nb = [g["batch"] for g in groups]
    meta_off = [0, nb[0]]
    n_meta_rows = sum(nb) * N_META
    te_meta = jnp.tile(_rope_table(jnp.arange(N_META)), (sum(nb), 1))
    hs = [x_prompt.reshape(-1, D_MODEL), x_sample.reshape(-1, D_MODEL),
          jnp.tile(meta_tokens.astype(F32), (sum(nb), 1))]
    tms = [ROW_TILE, ROW_TILE, n_meta_rows]

    for layer in range(DEPTH):
        hs = [_ffn(h, layer, *w["ffn1"], tm) for h, tm in zip(hs, tms)]
        proj = [_inproj(hs[g], layer, w, groups[g]["te"], groups[g]["seq"] // ROW_TILE, ROW_TILE) for g in range(2)]
        proj.append(_inproj(hs[2], layer, w, te_meta, 1, n_meta_rows))
        cb_m, ccu_m, nq_m, nk_m, nv_m, q_m, k_m, v_m = proj[2]
        yna_m = _na_meta(nq_m, nk_m, nv_m, w["na_mb"], layer)
        k_mp, v_mp = _pad_meta_rows(k_m), _pad_meta_rows(v_m)
        new_hs, ymla_m = [], []
        for g in range(2):
            cb, ccu, nq, nk, nv, q, k, v = proj[g]
            b, t = groups[g]["batch"], groups[g]["seq"]
            vid, st = _NA_BLOCKS[t // GRID_W]
            yna = _na(nq, nk, nv, nk_m, nv_m, w["na_bias"], layer,
                      jnp.asarray(vid), jnp.asarray(st), b, t, meta_off[g])
            ymla = _mla(q, k, v, k_mp, v_mp, b, t, meta_off[g], MLA_TQ, t, 0)
            ymla_m.append(_mla(q_m, k, v, k_mp, v_mp, b, t, meta_off[g], N_META, N_META, meta_off[g]))
            new_hs.append(_out(hs[g], cb, ccu, ccu, ccu, ccu_m, yna, ymla, layer, w, meta=False, tm=ROW_TILE,
                               seq=t, meta_off=meta_off[g]))
        new_hs.append(_out(hs[2], cb_m, ccu_m, proj[0][1], proj[1][1], ccu_m, yna_m, jnp.concatenate(ymla_m, axis=0),
                           layer, w, meta=True, tm=N_META, prompt_batches=nb[0],
                           seq_a=groups[0]["seq"], seq_b=groups[1]["seq"]))
        hs = new_hs
        last = layer == DEPTH - 1
        hs = [h if (last and n == 2) else _ffn(h, layer, *w["ffn2"], tm) for n, (h, tm) in enumerate(zip(hs, tms))]
    return (hs[0].reshape(x_prompt.shape), hs[1].reshape(x_sample.shape))
```

```python
import functools
import math

import numpy as np
import jax
import jax.numpy as jnp
from jax import lax
from jax.experimental import pallas as pl
from jax.experimental.pallas import tpu as pltpu

F32 = jnp.float32
BF16 = jnp.bfloat16

D_MODEL = 1024
DEPTH = 4
N_META = 16
GRID_W = 64
CONV_DIM = 256
NA_HEADS = 4
NA_HEAD_DIM = 64
NA_DIM = NA_HEADS * NA_HEAD_DIM
NA_MAX_WIN_H = 8
NA_WIN_W = 16
MLA_HEADS = 8
MLA_NOPE_DIM = 64
MLA_ROPE_DIM = 32
MLA_V_DIM = 64
MLA_Q_RANK = 768
MLA_KV_RANK = 256
MLA_DIM = MLA_HEADS * MLA_V_DIM
FFN_DIM = 2816
ROPE_THETA = 10000.0
RMS_EPS = 1e-6

LANES = 128
SUBLANES = 8
HEAD_SLOT = 128
IN_COLS = 6 * 256 + MLA_Q_RANK + MLA_KV_RANK + LANES
FFN_CHUNK = 256
ROW_TILE = 512
HALO_ROWS = 16
NA_QROWS = 2
NA_KROWS = NA_MAX_WIN_H + NA_QROWS - 1
NA_QB = NA_QROWS * GRID_W
NA_KB = NA_KROWS * GRID_W
NA_KCAT = 640
NA_UNROLL = 2
MLA_TQ = 256
MLA_TQ_T = 512
MLA_CK = 1024
MLA_AHEAD = 1
MASK_VALUE = -1e30
VMEM_LIMIT = 56 * 1024 * 1024


def _rms(x, g):
    ms = jnp.mean(x * x, axis=-1, keepdims=True)
    return x * lax.rsqrt(ms + RMS_EPS) * g


def _dot(a, b):
    return jnp.dot(a, b, preferred_element_type=F32)


def _dot_nt(a, b):
    return lax.dot_general(a, b, (((1,), (1,)), ((), ())), preferred_element_type=F32)


def _const_spec(block_shape, index):
    return pl.BlockSpec(block_shape, lambda *_: index, pipeline_mode=pl.Buffered(1))


def _params(n_axes, flags=None):
    return pltpu.CompilerParams(dimension_semantics=("arbitrary",) * n_axes,
                                vmem_limit_bytes=VMEM_LIMIT, flags=flags)


def _ffn_half_step(x, pre_ref, wgu_ref, wd_ref, post_ref):
    xn = _rms(x, pre_ref[...]).astype(BF16)
    acc = None
    for c in range(FFN_DIM // FFN_CHUNK):
        lo = c * FFN_CHUNK
        g = _dot(xn, wgu_ref[:, lo:lo + FFN_CHUNK])
        u = _dot(xn, wgu_ref[:, FFN_DIM + lo:FFN_DIM + lo + FFN_CHUNK])
        a = (g * jax.nn.sigmoid(g) * u).astype(BF16)
        d = _dot(a, wd_ref[lo:lo + FFN_CHUNK, :])
        acc = d if acc is None else acc + d
    return x + 0.5 * _rms(acc, post_ref[...])


def _ffn_specs(layer):
    return [_const_spec((None, 1, D_MODEL), (layer, 0, 0)),
            _const_spec((None, D_MODEL, 2 * FFN_DIM), (layer, 0, 0)),
            _const_spec((None, FFN_DIM, D_MODEL), (layer, 0, 0)),
            _const_spec((None, 1, D_MODEL), (layer, 0, 0))]


def _ffn_inproj_kernel(h_ref, fpre_ref, fwgu_ref, fwd_ref, fpost_ref,
                       pre_ref, win_ref, qn_ref, wuq_ref, kvn_ref, wuk_ref, wuv_ref, wuvt_ref, te_ref,
                       h_out_ref, cb_ref, ccu_ref, nq_ref, nk_ref, nv_ref, q_ref, k_ref, v_ref, vt_ref):
    tm = h_ref.shape[0]
    x = _ffn_half_step(h_ref[...], fpre_ref, fwgu_ref, fwd_ref, fpost_ref)
    h_out_ref[...] = x
    xn = _rms(x, pre_ref[...]).astype(BF16)
    z = _dot(xn, win_ref[...])
    cb_ref[...] = z[:, 0:256].astype(BF16)
    ccu_ref[...] = (z[:, 256:512] * z[:, 512:768]).astype(BF16)
    nq_ref[...] = (z[:, 768:1024] * (NA_HEAD_DIM ** -0.5)).astype(BF16)
    nk_ref[...] = z[:, 1024:1280].astype(BF16)
    nv_ref[...] = z[:, 1280:1536].astype(BF16)
    ql = _rms(z[:, 1536:1536 + MLA_Q_RANK], qn_ref[...]).astype(BF16)
    qf = _dot(ql, wuq_ref[...])
    kvl = _rms(z[:, 2304:2304 + MLA_KV_RANK], kvn_ref[...]).astype(BF16)
    kn = _dot(kvl, wuk_ref[...])
    v_ref[...] = _dot(kvl, wuv_ref[...]).astype(BF16)
    vt_ref[...] = _dot_nt(wuvt_ref[...], kvl).astype(BF16)
    te = te_ref[...]
    y = z[:, 2560:2560 + LANES] * te
    kr = y + pltpu.roll(y, MLA_ROPE_DIM, axis=1)
    low = lax.broadcasted_iota(jnp.int32, (tm, LANES), 1) < MLA_NOPE_DIM
    for h in range(MLA_HEADS):
        sl = slice(h * HEAD_SLOT, (h + 1) * HEAD_SLOT)
        qh = qf[:, sl] * _MLA_EXP2_SCALE
        qt = qh * te
        q_ref[:, sl] = (jnp.where(low, qh, qt) if h % 2 == 0 else jnp.where(low, qt, qh)).astype(BF16)
    for j in range(MLA_HEADS // 2):
        kp = kn[:, j * LANES:(j + 1) * LANES]
        k_ref[:, (2 * j) * HEAD_SLOT:(2 * j + 1) * HEAD_SLOT] = jnp.where(low, kp, kr).astype(BF16)
        k_ref[:, (2 * j + 1) * HEAD_SLOT:(2 * j + 2) * HEAD_SLOT] = jnp.where(low, kr, kp).astype(BF16)


def _ffn_inproj(h, layer, w, te, te_blocks, tm):
    n = h.shape[0]
    row = lambda c: pl.BlockSpec((tm, c), lambda i: (i, 0))
    outs = [(D_MODEL, F32), (256, BF16), (256, BF16), (256, BF16), (256, BF16), (256, BF16),
            (MLA_HEADS * HEAD_SLOT, BF16), (MLA_HEADS * HEAD_SLOT, BF16), (MLA_DIM, BF16)]
    return pl.pallas_call(
        _ffn_inproj_kernel,
        out_shape=[jax.ShapeDtypeStruct((n, c), dt) for c, dt in outs]
        + [jax.ShapeDtypeStruct((MLA_DIM, n), BF16)],
        grid=(n // tm,),
        in_specs=[row(D_MODEL)] + _ffn_specs(layer) + [
                  _const_spec((None, 1, D_MODEL), (layer, 0, 0)),
                  _const_spec((None, D_MODEL, IN_COLS), (layer, 0, 0)),
                  _const_spec((None, 1, MLA_Q_RANK), (layer, 0, 0)),
                  _const_spec((None, MLA_Q_RANK, MLA_HEADS * HEAD_SLOT), (layer, 0, 0)),
                  _const_spec((None, 1, MLA_KV_RANK), (layer, 0, 0)),
                  _const_spec((None, MLA_KV_RANK, MLA_HEADS * MLA_NOPE_DIM), (layer, 0, 0)),
                  _const_spec((None, MLA_KV_RANK, MLA_DIM), (layer, 0, 0)),
                  _const_spec((None, MLA_DIM, MLA_KV_RANK), (layer, 0, 0)),
                  pl.BlockSpec((tm, LANES), lambda i: (i % te_blocks, 0))],
        out_specs=[row(c) for c, _ in outs] + [pl.BlockSpec((MLA_DIM, tm), lambda i: (0, i))],
        compiler_params=_params(1),
        name="ffn_inproj",
    )(h, *w["ffn1"], w["mix_pre"], w["w_in"], w["q_norm"], w["w_uq"], w["kv_norm"], w["w_uk"], w["w_uv"], w["w_uv_t"], te)


def _softmax_pv(s_list, v_list):
    m = s_list[0].max(axis=-1, keepdims=True)
    for s in s_list[1:]:
        m = jnp.maximum(m, s.max(axis=-1, keepdims=True))
    den = None
    out = None
    for s, v in zip(s_list, v_list):
        p = jnp.exp(s - m)
        l = p.sum(axis=-1, keepdims=True)
        o = _dot(p.astype(BF16), v)
        den = l if den is None else den + l
        out = o if out is None else out + o
    return out / den


def _na_kernel(vid_ref, st_ref, q_ref, k_ref, v_ref, km_ref, vm_ref, *rest):
    del vid_ref
    bias_refs, (o_ref, kcat, vcat) = rest[:NA_UNROLL], rest[NA_UNROLL:]
    i = pl.program_id(1)
    low = lax.broadcasted_iota(jnp.int32, (NA_QB, LANES), 1) < NA_HEAD_DIM
    zero_rows = jnp.zeros((NA_KCAT - NA_KB - N_META, NA_DIM), BF16)
    for u in range(NA_UNROLL):
        start = pl.multiple_of(st_ref[i * NA_UNROLL + u] * GRID_W, GRID_W)
        for cat, win, meta in ((kcat, k_ref, km_ref), (vcat, v_ref, vm_ref)):
            cat[u, 0:NA_KB, :] = win[pl.ds(start, NA_KB), :]
            cat[u, NA_KB:NA_KB + N_META, :] = meta[...]
            cat[u, NA_KB + N_META:, :] = zero_rows
    chains = [(u, j) for u in range(NA_UNROLL) for j in range(NA_HEADS // 2)]
    sl = lambda j: slice(j * LANES, (j + 1) * LANES)
    rows = lambda u: slice(u * NA_QB, (u + 1) * NA_QB)

    def scores(u, j):
        qp = q_ref[rows(u), sl(j)]
        q2 = jnp.concatenate([jnp.where(low, qp, jnp.zeros_like(qp)),
                              jnp.where(low, jnp.zeros_like(qp), qp)], axis=0)
        return _dot_nt(q2, kcat[u, :, sl(j)]) + bias_refs[u][j]

    s_next = scores(*chains[0])
    for n, (u, j) in enumerate(chains):
        s = s_next
        if n + 1 < len(chains):
            s_next = scores(*chains[n + 1])
        p = jnp.exp(s - s.max(axis=-1, keepdims=True))
        o = _dot(p.astype(BF16), vcat[u, :, sl(j)]) / p.sum(axis=-1, keepdims=True)
        o_ref[rows(u), sl(j)] = jnp.where(low, o[:NA_QB], o[NA_QB:]).astype(BF16)


def _na(nq, nk, nv, nkm, nvm, bias, layer, vid, st, batch, seq, meta_off):
    nstep = seq // (NA_QB * NA_UNROLL)
    n_patterns = bias.shape[0] // DEPTH
    qspec = pl.BlockSpec((NA_QB * NA_UNROLL, NA_DIM), lambda b, i, vid, st: (b * nstep + i, 0))
    kvspec = pl.BlockSpec((seq, NA_DIM), lambda b, i, vid, st: (b, 0))
    mspec = pl.BlockSpec((N_META, NA_DIM), lambda b, i, vid, st: (meta_off + b, 0))
    bias_spec = lambda u: pl.BlockSpec(
        (None, NA_HEADS // 2, 2 * NA_QB, NA_KCAT),
        lambda b, i, vid, st: (layer * n_patterns + vid[i * NA_UNROLL + u], 0, 0, 0))
    grid_spec = pltpu.PrefetchScalarGridSpec(
        num_scalar_prefetch=2,
        grid=(batch, nstep),
        in_specs=[qspec, kvspec, kvspec, mspec, mspec] + [bias_spec(u) for u in range(NA_UNROLL)],
        out_specs=qspec,
        scratch_shapes=[pltpu.VMEM((NA_UNROLL, NA_KCAT, NA_DIM), BF16)] * 2)
    return pl.pallas_call(
        _na_kernel,
        out_shape=jax.ShapeDtypeStruct((batch * seq, NA_DIM), BF16),
        grid_spec=grid_spec,
        compiler_params=_params(2),
        name="na",
    )(vid, st, nq, nk, nv, nkm, nvm, *([bias] * NA_UNROLL))


def _na_meta_kernel(q_ref, k_ref, v_ref, mb_ref, o_ref):
    low = lax.broadcasted_iota(jnp.int32, (N_META, LANES), 1) < NA_HEAD_DIM
    for j in range(NA_HEADS // 2):
        sl = slice(j * LANES, (j + 1) * LANES)
        qp = q_ref[:, sl]
        halves = []
        for half in range(2):
            h = 2 * j + half
            qh = jnp.where(low if half == 0 else jnp.logical_not(low), qp, jnp.zeros_like(qp))
            s = _dot_nt(qh, k_ref[:, sl]) + mb_ref[h:h + 1, :]
            halves.append(_softmax_pv([s], [v_ref[:, sl]]))
        o_ref[:, sl] = jnp.where(low, halves[0], halves[1]).astype(BF16)


def _na_meta(nqm, nkm, nvm, mb, layer):
    n = nqm.shape[0]
    spec = pl.BlockSpec((N_META, NA_DIM), lambda b: (b, 0))
    return pl.pallas_call(
        _na_meta_kernel,
        out_shape=jax.ShapeDtypeStruct((n, NA_DIM), BF16),
        grid=(n // N_META,),
        in_specs=[spec, spec, spec, pl.BlockSpec((None, NA_HEADS, N_META), lambda b: (layer, 0, 0))],
        out_specs=spec,
        compiler_params=_params(1),
        name="na_meta",
    )(nqm, nkm, nvm, mb)


_MLA_EXP2_SCALE = (MLA_NOPE_DIM + MLA_ROPE_DIM) ** -0.5 * math.log2(math.e)


def _mla_kernel(q_ref, k_ref, v_ref, km_ref, vm_ref, o_ref, m_scr, l_scr, acc_scr, *, seq):
    tq = q_ref.shape[0]
    hsl = lambda h: slice(h * HEAD_SLOT, (h + 1) * HEAD_SLOT)
    vsl = lambda h: slice((h // 2) * LANES, (h // 2 + 1) * LANES)
    tiles = lambda x: [x[:, t * LANES:(t + 1) * LANES] for t in range(x.shape[1] // LANES)]

    def softmax_piece(s, m_old):
        parts = tiles(s)
        lane_max = functools.reduce(jnp.maximum, parts)
        mn = jnp.broadcast_to(lane_max.max(axis=-1, keepdims=True), (tq, LANES))
        if m_old is not None:
            mn = jnp.maximum(m_old, mn)
        ps = [jnp.exp2(part - mn) for part in parts]
        return mn, jnp.concatenate([p.astype(BF16) for p in ps], axis=-1), functools.reduce(jnp.add, ps)

    col = lax.broadcasted_iota(jnp.int32, (tq, LANES), 1)
    pad_mask = jnp.where(col < N_META, 0.0, MASK_VALUE).astype(F32)

    def chunk(off, first):
        def scores(h):
            s = _dot_nt(q_ref[:, hsl(h)], k_ref[pl.ds(off, MLA_CK), hsl(h)])
            if first:
                s = jnp.concatenate([s, _dot_nt(q_ref[:, hsl(h)], km_ref[:, hsl(h)]) + pad_mask], axis=-1)
            return s

        def values(p, h):
            o = _dot(p[:, :MLA_CK], v_ref[pl.ds(off, MLA_CK), vsl(h)])
            return o + _dot(p[:, MLA_CK:], vm_ref[:, vsl(h)]) if first else o

        ahead = [scores(h) for h in range(MLA_AHEAD)]
        for h in range(MLA_HEADS):
            s = ahead.pop(0)
            if h + MLA_AHEAD < MLA_HEADS:
                ahead.append(scores(h + MLA_AHEAD))
            if first:
                m_scr[h], p, l_scr[h] = softmax_piece(s, None)
                acc_scr[h] = values(p, h)
            else:
                m = m_scr[h]
                mn, p, lsum = softmax_piece(s, m)
                alpha = jnp.exp2(m - mn)
                m_scr[h] = mn
                l_scr[h] = alpha * l_scr[h] + lsum
                acc_scr[h] = alpha * acc_scr[h] + values(p, h)

    for ci in range(seq // MLA_CK):
        chunk(ci * MLA_CK, ci == 0)
    low = col < MLA_V_DIM
    out = [acc_scr[h] / l_scr[h].sum(axis=-1, keepdims=True) for h in range(MLA_HEADS)]
    for j in range(MLA_HEADS // 2):
        o_ref[:, vsl(2 * j)] = jnp.where(low, out[2 * j], out[2 * j + 1]).astype(BF16)


def _mla(q, k, v, km, vm, batch, seq, meta_off, tq, q_rows_per_batch, q_off):
    nq = q_rows_per_batch // tq
    qspec = lambda c: pl.BlockSpec((tq, c), lambda b, i: (q_off + b * nq + i, 0))
    kvspec = lambda c: pl.BlockSpec((seq, c), lambda b, i: (b, 0))
    mspec = lambda c: pl.BlockSpec((LANES, c), lambda b, i: (meta_off + b, 0))
    ospec = pl.BlockSpec((tq, MLA_DIM), lambda b, i: (b * nq + i, 0))
    return pl.pallas_call(
        functools.partial(_mla_kernel, seq=seq),
        out_shape=jax.ShapeDtypeStruct((batch * q_rows_per_batch, MLA_DIM), BF16),
        grid=(batch, nq),
        in_specs=[qspec(MLA_HEADS * HEAD_SLOT), kvspec(MLA_HEADS * HEAD_SLOT), kvspec(MLA_DIM),
                  mspec(MLA_HEADS * HEAD_SLOT), mspec(MLA_DIM)],
        out_specs=ospec,
        scratch_shapes=[pltpu.VMEM((MLA_HEADS, tq, LANES), F32)] * 3,
        compiler_params=_params(2),
        name="mla",
    )(q, k, v, km, vm)


def _mla_t_kernel(q_ref, k_ref, vt_ref, km_ref, vmt_ref, o_ref, m_scr, l_scr, acc_scr, *, seq):
    tq = q_ref.shape[0]
    hsl = lambda h: slice(h * HEAD_SLOT, (h + 1) * HEAD_SLOT)
    dsl = lambda h: slice(h * MLA_V_DIM, (h + 1) * MLA_V_DIM)
    by_sublane = lambda x: x.reshape(x.shape[0] // SUBLANES, SUBLANES, tq)
    key_row = lax.broadcasted_iota(jnp.int32, (LANES, tq), 0)
    pad_mask = jnp.where(key_row < N_META, 0.0, MASK_VALUE).astype(F32)

    def softmax_piece(s, m_old):
        mn = by_sublane(s).max(axis=0).max(axis=0, keepdims=True)
        if m_old is not None:
            mn = jnp.maximum(m_old, mn)
        p = jnp.exp2(s - mn)
        return mn, p.astype(BF16), by_sublane(p).sum(axis=0)

    def chunk(off, first):
        def scores(h):
            q = q_ref[:, hsl(h)]
            s = _dot_nt(k_ref[off:off + MLA_CK, hsl(h)], q)
            if first:
                s = jnp.concatenate([s, _dot_nt(km_ref[:, hsl(h)], q) + pad_mask], axis=0)
            return s

        def values(p, h):
            o = _dot(vt_ref[dsl(h), off:off + MLA_CK], p[:MLA_CK])
            return o + _dot(vmt_ref[dsl(h), :], p[MLA_CK:]) if first else o

        s_next = scores(0)
        for h in range(MLA_HEADS):
            s = s_next
            if h + 1 < MLA_HEADS:
                s_next = scores(h + 1)
            if first:
                mn, p, lsum = softmax_piece(s, None)
                l_scr[h] = lsum
                acc_scr[h] = values(p, h)
            else:
                m = m_scr[h][0:1, :]
                mn, p, lsum = softmax_piece(s, m)
                alpha = jnp.exp2(m - mn)
                l_scr[h] = alpha * l_scr[h] + lsum
                acc_scr[h] = alpha * acc_scr[h] + values(p, h)
            m_scr[h] = jnp.broadcast_to(mn, (SUBLANES, tq))

    for ci in range(seq // MLA_CK):
        chunk(ci * MLA_CK, ci == 0)
    out_t = [acc_scr[h] / l_scr[h].sum(axis=0, keepdims=True) for h in range(MLA_HEADS)]
    o_ref[...] = jnp.concatenate(out_t, axis=0).T.astype(BF16)


def _mla_t(q, k, vt, km, vmt, batch, seq, meta_off):
    nq = seq // MLA_TQ_T
    qspec = pl.BlockSpec((MLA_TQ_T, MLA_HEADS * HEAD_SLOT), lambda b, i: (b * nq + i, 0))
    return pl.pallas_call(
        functools.partial(_mla_t_kernel, seq=seq),
        out_shape=jax.ShapeDtypeStruct((batch * seq, MLA_DIM), BF16),
        grid=(batch, nq),
        in_specs=[qspec,
                  pl.BlockSpec((seq, MLA_HEADS * HEAD_SLOT), lambda b, i: (b, 0)),
                  pl.BlockSpec((MLA_DIM, seq), lambda b, i: (0, b)),
                  pl.BlockSpec((LANES, MLA_HEADS * HEAD_SLOT), lambda b, i: (meta_off + b, 0)),
                  pl.BlockSpec((MLA_DIM, LANES), lambda b, i: (0, meta_off + b))],
        out_specs=pl.BlockSpec((MLA_TQ_T, MLA_DIM), lambda b, i: (b * nq + i, 0)),
        scratch_shapes=[pltpu.VMEM((MLA_HEADS, SUBLANES, MLA_TQ_T), F32),
                        pltpu.VMEM((MLA_HEADS, SUBLANES, MLA_TQ_T), F32),
                        pltpu.VMEM((MLA_HEADS, MLA_V_DIM, MLA_TQ_T), F32)],
        compiler_params=_params(2),
        name="mla_t",
    )(q, k, vt, km, vmt)


def _mix_ffn_kernel(h_ref, cb_ref, ccu_ref, halo_a_ref, halo_b_ref, halo_c_ref, yna_ref, ymla_ref,
                    convw_ref, cn_ref, nn_ref, mn_ref, wo_ref, post_ref,
                    fpre_ref, fwgu_ref, fwd_ref, fpost_ref, o_ref, *, meta, tiles_per_seq, prompt_batches):
    tm = h_ref.shape[0]
    i = pl.program_id(0)
    ccu = ccu_ref[...].astype(F32)
    first_row = lambda ref: ref[...].astype(F32)[0:1, :]
    last_row = lambda ref: ref[...].astype(F32)[HALO_ROWS - 1:HALO_ROWS, :]
    if meta:
        prev_row = jnp.zeros((1, CONV_DIM), F32)
        next_row = jnp.where(i < prompt_batches, first_row(halo_a_ref), first_row(halo_b_ref))
    else:
        t = i % tiles_per_seq
        prev_row = jnp.where(t == 0, last_row(halo_c_ref), last_row(halo_a_ref))
        next_row = jnp.where(t == tiles_per_seq - 1, jnp.zeros((1, CONV_DIM), F32), first_row(halo_b_ref))
    row = lax.broadcasted_iota(jnp.int32, (tm, CONV_DIM), 0)
    dn = jnp.where(row == 0, prev_row, pltpu.roll(ccu, 1, axis=0))
    up = jnp.where(row == tm - 1, next_row, pltpu.roll(ccu, tm - 1, axis=0))
    w = convw_ref[...]
    yc = cb_ref[...].astype(F32) * (w[0:1, :] * dn + w[1:2, :] * ccu + w[2:3, :] * up)
    y = jnp.concatenate([_rms(yc, cn_ref[...]), _rms(yna_ref[...].astype(F32), nn_ref[...]),
                         _rms(ymla_ref[...].astype(F32), mn_ref[...])], axis=-1).astype(BF16)
    x = h_ref[...] + _rms(_dot(y, wo_ref[...]), post_ref[...])
    o_ref[...] = _ffn_half_step(x, fpre_ref, fwgu_ref, fwd_ref, fpost_ref)


def _mix_ffn(h, cb, ccu, halo_a, halo_b, halo_c, yna, ymla, layer, w, *, meta, tm, seq=None,
         meta_off=0, prompt_batches=0, seq_a=0, seq_b=0):
    n = h.shape[0]
    row = lambda c: pl.BlockSpec((tm, c), lambda i: (i, 0))
    halo = lambda index_map: pl.BlockSpec((HALO_ROWS, CONV_DIM), index_map)
    if meta:
        tiles_per_seq = 1
        a_spec = halo(lambda i: (jnp.minimum(i, prompt_batches - 1) * (seq_a // HALO_ROWS), 0))
        b_spec = halo(lambda i: (jnp.maximum(i - prompt_batches, 0) * (seq_b // HALO_ROWS), 0))
        c_spec = halo(lambda i: (0, 0))
    else:
        tiles_per_seq = seq // tm
        last = n // HALO_ROWS - 1
        a_spec = halo(lambda i: (jnp.maximum(i * (tm // HALO_ROWS) - 1, 0), 0))
        b_spec = halo(lambda i: (jnp.minimum((i + 1) * (tm // HALO_ROWS), last), 0))
        c_spec = halo(lambda i: (meta_off + i // tiles_per_seq, 0))
    return pl.pallas_call(
        functools.partial(_mix_ffn_kernel, meta=meta, tiles_per_seq=tiles_per_seq, prompt_batches=prompt_batches),
        out_shape=jax.ShapeDtypeStruct(h.shape, F32),
        grid=(n // tm,),
        in_specs=[row(D_MODEL), row(CONV_DIM), row(CONV_DIM), a_spec, b_spec, c_spec,
                  row(NA_DIM), row(MLA_DIM),
                  _const_spec((None, 3, CONV_DIM), (layer, 0, 0)),
                  _const_spec((None, 1, CONV_DIM), (layer, 0, 0)),
                  _const_spec((None, 1, NA_DIM), (layer, 0, 0)),
                  _const_spec((None, 1, MLA_DIM), (layer, 0, 0)),
                  _const_spec((None, D_MODEL, D_MODEL), (layer, 0, 0)),
                  _const_spec((None, 1, D_MODEL), (layer, 0, 0))] + _ffn_specs(layer),
        out_specs=row(D_MODEL),
        compiler_params=_params(1),
        name="mix_ffn",
    )(h, cb, ccu, halo_a, halo_b, halo_c, yna, ymla,
      w["conv_w"], w["conv_on"], w["na_on"], w["mla_on"], w["w_o"], w["mix_post"], *w["ffn2"])


def _na_patterns():
    patterns = []
    per_rows = {}
    for rows in (2048 // GRID_W, 4096 // GRID_W):
        win_h = min(NA_MAX_WIN_H, rows)
        vids, starts = [], []
        for blk in range(rows // NA_QROWS):
            r0 = blk * NA_QROWS
            start = int(np.clip(r0 - win_h // 2, 0, rows - NA_KROWS))
            qr = r0 + np.arange(NA_QROWS)[:, None]
            kr = start + np.arange(NA_KROWS)[None, :]
            rs = np.clip(qr - win_h // 2, 0, rows - win_h)
            inside = (kr >= rs) & (kr < rs + win_h)
            assert (inside.sum(axis=1) == win_h).all()
            rel = np.where(inside, kr - qr + NA_MAX_WIN_H - 1, -1)
            for n, p in enumerate(patterns):
                if np.array_equal(p, rel):
                    vids.append(n)
                    break
            else:
                vids.append(len(patterns))
                patterns.append(rel)
            starts.append(start)
        per_rows[rows] = (np.asarray(vids, np.int32), np.asarray(starts, np.int32))
    return np.stack(patterns), per_rows


_NA_REL_ROWS, _NA_BLOCKS = _na_patterns()


def _na_col_tables():
    qc = np.arange(GRID_W)[:, None]
    kc = np.arange(GRID_W)[None, :]
    cs = np.clip(qc - NA_WIN_W // 2, 0, GRID_W - NA_WIN_W)
    inside = (kc >= cs) & (kc < cs + NA_WIN_W)
    rel = kc - qc + NA_WIN_W - 1
    onehot = (rel[None] == np.arange(2 * NA_WIN_W - 1)[:, None, None]) & inside[None]
    return onehot.astype(np.float32), inside


_NA_COL_ONEHOT, _NA_COL_INSIDE = _na_col_tables()


def _na_bias_tables(rpb, meta_bias):
    blocks = jnp.einsum("lhrd,dqk->lhrqk", rpb.astype(F32), _NA_COL_ONEHOT, precision=lax.Precision.HIGHEST)
    blocks = jnp.where(_NA_COL_INSIDE, blocks, MASK_VALUE)
    masked = jnp.full(blocks.shape[:2] + (GRID_W, GRID_W), MASK_VALUE, F32)
    meta_cols = jnp.broadcast_to(meta_bias.astype(F32)[:, :, None, :], meta_bias.shape[:2] + (GRID_W, N_META))
    pad_cols = jnp.full(meta_bias.shape[:2] + (GRID_W, NA_KCAT - NA_KB - N_META), MASK_VALUE, F32)
    tables = []
    for rel in _NA_REL_ROWS:
        rows = [jnp.concatenate([masked if r < 0 else blocks[:, :, int(r)] for r in rel_q] + [meta_cols, pad_cols],
                                axis=-1) for rel_q in rel]
        tables.append(jnp.concatenate(rows, axis=-2))
    return jnp.stack(tables, axis=1).reshape(-1, NA_HEADS // 2, 2 * NA_QB, NA_KCAT)


def _rope_table(pos):
    half = MLA_ROPE_DIM // 2
    inv_freq = ROPE_THETA ** (-jnp.arange(half, dtype=F32) / half)
    ang = pos.astype(F32)[:, None] * inv_freq[None, :]
    cos = jnp.concatenate([jnp.cos(ang)] * 2, axis=-1)
    sin = jnp.concatenate([jnp.sin(ang)] * 2, axis=-1)
    return jnp.concatenate([cos, sin, cos, sin], axis=-1)


def _rot_cols(w):
    half = MLA_ROPE_DIM // 2
    return jnp.concatenate([-w[..., half:], w[..., :half]], axis=-1)


def _prep_weights(ffn1_pre, ffn1_gu, ffn1_down, ffn1_post, mix_pre, w_in, conv_w, na_rpb, na_mb, q_norm, w_uq,
                  kv_norm, w_ukv, conv_on, na_on, mla_on, w_o, mix_post, ffn2_pre, ffn2_gu, ffn2_down, ffn2_post):
    vec = lambda g: g[:, None, :]
    kpe = w_in[:, :, 2560:2560 + MLA_ROPE_DIM]
    kpe_rot = _rot_cols(kpe)
    w_in_ext = jnp.concatenate([w_in[:, :, :2560], kpe, kpe_rot, kpe, kpe_rot], axis=-1).astype(BF16)
    per_head = w_uq.reshape(DEPTH, MLA_Q_RANK, MLA_HEADS, MLA_NOPE_DIM + MLA_ROPE_DIM)
    nope, pe = per_head[..., :MLA_NOPE_DIM], per_head[..., MLA_NOPE_DIM:]
    even = jnp.concatenate([nope, pe, _rot_cols(pe)], axis=-1)
    odd = jnp.concatenate([pe, _rot_cols(pe), nope], axis=-1)
    is_even = (jnp.arange(MLA_HEADS) % 2 == 0)[None, None, :, None]
    w_uq_ext = jnp.where(is_even, even, odd).reshape(DEPTH, MLA_Q_RANK, MLA_HEADS * HEAD_SLOT).astype(BF16)
    kv_heads = w_ukv.reshape(DEPTH, MLA_KV_RANK, MLA_HEADS, MLA_NOPE_DIM + MLA_V_DIM)
    w_uk = kv_heads[..., :MLA_NOPE_DIM].reshape(DEPTH, MLA_KV_RANK, MLA_HEADS * MLA_NOPE_DIM).astype(BF16)
    w_uv = kv_heads[..., MLA_NOPE_DIM:].reshape(DEPTH, MLA_KV_RANK, MLA_DIM).astype(BF16)
    bias = _na_bias_tables(na_rpb, na_mb)
    return dict(
        ffn1=(vec(ffn1_pre), ffn1_gu.astype(BF16), ffn1_down.astype(BF16), vec(ffn1_post)),
        ffn2=(vec(ffn2_pre), ffn2_gu.astype(BF16), ffn2_down.astype(BF16), vec(ffn2_post)),
        mix_pre=vec(mix_pre), w_in=w_in_ext, q_norm=vec(q_norm), w_uq=w_uq_ext, kv_norm=vec(kv_norm),
        w_uk=w_uk, w_uv=w_uv, w_uv_t=jnp.swapaxes(w_uv, 1, 2), conv_w=conv_w, conv_on=vec(conv_on), na_on=vec(na_on), mla_on=vec(mla_on),
        w_o=w_o.astype(BF16), mix_post=vec(mix_post), na_bias=bias, na_mb=na_mb)


def _pad_meta_rows(x):
    c = x.shape[-1]
    x = x.reshape(-1, N_META, c)
    return jnp.pad(x, ((0, 0), (0, LANES - N_META), (0, 0))).reshape(-1, c)


def kernel(x_prompt, x_sample, meta_tokens, ffn1_pre_norm, ffn1_w_gu, ffn1_w_down, ffn1_post_norm, mix_pre_norm,
           w_in, conv_w, na_rpb, na_meta_bias, mla_q_norm, mla_w_uq, mla_kv_norm, mla_w_ukv, conv_out_norm,
           na_out_norm, mla_out_norm, w_o, mix_post_norm, ffn2_pre_norm, ffn2_w_gu, ffn2_w_down, ffn2_post_norm):
    w = _prep_weights(ffn1_pre_norm, ffn1_w_gu, ffn1_w_down, ffn1_post_norm, mix_pre_norm, w_in, conv_w, na_rpb,
                      na_meta_bias, mla_q_norm, mla_w_uq, mla_kv_norm, mla_w_ukv, conv_out_norm, na_out_norm,
                      mla_out_norm, w_o, mix_post_norm, ffn2_pre_norm, ffn2_w_gu, ffn2_w_down, ffn2_post_norm)
    groups = []
    for x in (x_prompt, x_sample):
        b, t, _ = x.shape
        groups.append(dict(batch=b, seq=t, te=_rope_table(N_META + jnp.arange(t))))
    nb = [g["batch"] for g in groups]
    meta_off = [0, nb[0]]
    n_meta_rows = sum(nb) * N_META
    te_meta = jnp.tile(_rope_table(jnp.arange(N_META)), (sum(nb), 1))
    hs = [x_prompt.reshape(-1, D_MODEL), x_sample.reshape(-1, D_MODEL),
          jnp.tile(meta_tokens.astype(F32), (sum(nb), 1))]

    for layer in range(DEPTH):
        proj = [_ffn_inproj(hs[g], layer, w, groups[g]["te"], groups[g]["seq"] // ROW_TILE, ROW_TILE)
                for g in range(2)]
        proj.append(_ffn_inproj(hs[2], layer, w, te_meta, 1, n_meta_rows))
        hs = [p[0] for p in proj]
        cb_m, ccu_m, nq_m, nk_m, nv_m, q_m, k_m, v_m, vt_m = proj[2][1:]
        k_mp, v_mp = _pad_meta_rows(k_m), _pad_meta_rows(v_m)
        vt_mp = jnp.pad(vt_m.reshape(MLA_DIM, -1, N_META),
                        ((0, 0), (0, 0), (0, LANES - N_META))).reshape(MLA_DIM, -1)
        meta_live = layer < DEPTH - 1
        new_hs, ymla_m = [], []
        for g in range(2):
            cb, ccu, nq, nk, nv, q, k, v, vt = proj[g][1:]
            b, t = groups[g]["batch"], groups[g]["seq"]
            vid, st = _NA_BLOCKS[t // GRID_W]
            yna = _na(nq, nk, nv, nk_m, nv_m, w["na_bias"], layer,
                      jnp.asarray(vid), jnp.asarray(st), b, t, meta_off[g])
            ymla = _mla_t(q, k, vt, k_mp, vt_mp, b, t, meta_off[g])
            if meta_live:
                ymla_m.append(_mla(q_m, k, v, k_mp, v_mp, b, t, meta_off[g], N_META, N_META, meta_off[g]))
            new_hs.append(_mix_ffn(hs[g], cb, ccu, ccu, ccu, ccu_m, yna, ymla, layer, w, meta=False, tm=ROW_TILE,
                                   seq=t, meta_off=meta_off[g]))
        if meta_live:
            yna_m = _na_meta(nq_m, nk_m, nv_m, w["na_mb"], layer)
            new_hs.append(_mix_ffn(hs[2], cb_m, ccu_m, proj[0][2], proj[1][2], ccu_m, yna_m,
                                   jnp.concatenate(ymla_m, axis=0), layer, w, meta=True, tm=N_META,
                                   prompt_batches=nb[0], seq_a=groups[0]["seq"], seq_b=groups[1]["seq"]))
        hs = new_hs
    return (hs[0].reshape(x_prompt.shape), hs[1].reshape(x_sample.shape))
```

```python
import functools
import math

import numpy as np
import jax
import jax.numpy as jnp
from jax import lax
from jax.experimental import pallas as pl
from jax.experimental.pallas import tpu as pltpu

F32 = jnp.float32
BF16 = jnp.bfloat16

D_MODEL = 1024
DEPTH = 4
N_META = 16
GRID_W = 64
CONV_DIM = 256
NA_HEADS = 4
NA_HEAD_DIM = 64
NA_DIM = NA_HEADS * NA_HEAD_DIM
NA_MAX_WIN_H = 8
NA_WIN_W = 16
MLA_HEADS = 8
MLA_NOPE_DIM = 64
MLA_ROPE_DIM = 32
MLA_V_DIM = 64
MLA_Q_RANK = 768
MLA_KV_RANK = 256
MLA_DIM = MLA_HEADS * MLA_V_DIM
FFN_DIM = 2816
ROPE_THETA = 10000.0
RMS_EPS = 1e-6

LANES = 128
HEAD_SLOT = 128
IN_COLS = 6 * 256 + MLA_Q_RANK + MLA_KV_RANK + LANES
FFN_CHUNK = 256
ROW_TILE = 512
HALO_ROWS = 16
NA_QROWS = 2
NA_KROWS = NA_MAX_WIN_H + NA_QROWS - 1
NA_QB = NA_QROWS * GRID_W
NA_KB = NA_KROWS * GRID_W
NA_KCAT = 640
NA_UNROLL = 8
MLA_TQ = 256
MLA_CK = 1024
MLA_AHEAD = 1
MASK_VALUE = -1e30
LOG2_E = math.log2(math.e)
VMEM_LIMIT = 56 * 1024 * 1024


def _rms(x, g):
    ms = jnp.mean(x * x, axis=-1, keepdims=True)
    return x * lax.rsqrt(ms + RMS_EPS) * g


def _dot(a, b):
    return jnp.dot(a, b, preferred_element_type=F32)


def _dot_nt(a, b):
    return lax.dot_general(a, b, (((1,), (1,)), ((), ())), preferred_element_type=F32)


def _const_spec(block_shape, index):
    return pl.BlockSpec(block_shape, lambda *_: index, pipeline_mode=pl.Buffered(1))


def _params(n_axes, flags=None):
    return pltpu.CompilerParams(dimension_semantics=("arbitrary",) * n_axes,
                                vmem_limit_bytes=VMEM_LIMIT, flags=flags)


def _ffn_half_step(x, pre_ref, wgu_ref, wd_ref, post_ref):
    xn = _rms(x, pre_ref[...]).astype(BF16)
    acc = None
    for c in range(FFN_DIM // FFN_CHUNK):
        lo = c * FFN_CHUNK
        g = _dot(xn, wgu_ref[:, lo:lo + FFN_CHUNK])
        u = _dot(xn, wgu_ref[:, FFN_DIM + lo:FFN_DIM + lo + FFN_CHUNK])
        a = (g * jax.nn.sigmoid(g) * u).astype(BF16)
        d = _dot(a, wd_ref[lo:lo + FFN_CHUNK, :])
        acc = d if acc is None else acc + d
    return x + 0.5 * _rms(acc, post_ref[...])


def _ffn_specs(layer):
    return [_const_spec((None, 1, D_MODEL), (layer, 0, 0)),
            _const_spec((None, D_MODEL, 2 * FFN_DIM), (layer, 0, 0)),
            _const_spec((None, FFN_DIM, D_MODEL), (layer, 0, 0)),
            _const_spec((None, 1, D_MODEL), (layer, 0, 0))]


def _ffn_inproj_kernel(h_ref, fpre_ref, fwgu_ref, fwd_ref, fpost_ref,
                       pre_ref, win_ref, qn_ref, wuq_ref, kvn_ref, wuk_ref, wuv_ref, te_ref,
                       h_out_ref, cb_ref, ccu_ref, nq_ref, nk_ref, nv_ref, q_ref, k_ref, v_ref):
    tm = h_ref.shape[0]
    x = _ffn_half_step(h_ref[...], fpre_ref, fwgu_ref, fwd_ref, fpost_ref)
    h_out_ref[...] = x
    xn = _rms(x, pre_ref[...]).astype(BF16)
    z = _dot(xn, win_ref[...])
    cb_ref[...] = z[:, 0:256].astype(BF16)
    ccu_ref[...] = (z[:, 256:512] * z[:, 512:768]).astype(BF16)
    nq_ref[...] = (z[:, 768:1024] * (NA_HEAD_DIM ** -0.5 * LOG2_E)).astype(BF16)
    nk_ref[...] = z[:, 1024:1280].astype(BF16)
    nv_ref[...] = z[:, 1280:1536].astype(BF16)
    ql = _rms(z[:, 1536:1536 + MLA_Q_RANK], qn_ref[...]).astype(BF16)
    qf = _dot(ql, wuq_ref[...])
    kvl = _rms(z[:, 2304:2304 + MLA_KV_RANK], kvn_ref[...]).astype(BF16)
    kn = _dot(kvl, wuk_ref[...])
    v_ref[...] = _dot(kvl, wuv_ref[...]).astype(BF16)
    te = te_ref[...]
    y = z[:, 2560:2560 + LANES] * te
    kr = y + pltpu.roll(y, MLA_ROPE_DIM, axis=1)
    low = lax.broadcasted_iota(jnp.int32, (tm, LANES), 1) < MLA_NOPE_DIM
    for h in range(MLA_HEADS):
        sl = slice(h * HEAD_SLOT, (h + 1) * HEAD_SLOT)
        qh = qf[:, sl] * _MLA_EXP2_SCALE
        qt = qh * te
        q_ref[:, sl] = (jnp.where(low, qh, qt) if h % 2 == 0 else jnp.where(low, qt, qh)).astype(BF16)
    for j in range(MLA_HEADS // 2):
        kp = kn[:, j * LANES:(j + 1) * LANES]
        k_ref[:, (2 * j) * HEAD_SLOT:(2 * j + 1) * HEAD_SLOT] = jnp.where(low, kp, kr).astype(BF16)
        k_ref[:, (2 * j + 1) * HEAD_SLOT:(2 * j + 2) * HEAD_SLOT] = jnp.where(low, kr, kp).astype(BF16)


def _ffn_inproj(h, layer, w, te, te_blocks, tm):
    n = h.shape[0]
    row = lambda c: pl.BlockSpec((tm, c), lambda i: (i, 0))
    outs = [(D_MODEL, F32), (256, BF16), (256, BF16), (256, BF16), (256, BF16), (256, BF16),
            (MLA_HEADS * HEAD_SLOT, BF16), (MLA_HEADS * HEAD_SLOT, BF16), (MLA_DIM, BF16)]
    return pl.pallas_call(
        _ffn_inproj_kernel,
        out_shape=[jax.ShapeDtypeStruct((n, c), dt) for c, dt in outs],
        grid=(n // tm,),
        in_specs=[row(D_MODEL)] + _ffn_specs(layer) + [
                  _const_spec((None, 1, D_MODEL), (layer, 0, 0)),
                  _const_spec((None, D_MODEL, IN_COLS), (layer, 0, 0)),
                  _const_spec((None, 1, MLA_Q_RANK), (layer, 0, 0)),
                  _const_spec((None, MLA_Q_RANK, MLA_HEADS * HEAD_SLOT), (layer, 0, 0)),
                  _const_spec((None, 1, MLA_KV_RANK), (layer, 0, 0)),
                  _const_spec((None, MLA_KV_RANK, MLA_HEADS * MLA_NOPE_DIM), (layer, 0, 0)),
                  _const_spec((None, MLA_KV_RANK, MLA_DIM), (layer, 0, 0)),
                  pl.BlockSpec((tm, LANES), lambda i: (i % te_blocks, 0))],
        out_specs=[row(c) for c, _ in outs],
        compiler_params=_params(1),
        name="ffn_inproj",
    )(h, *w["ffn1"], w["mix_pre"], w["w_in"], w["q_norm"], w["w_uq"], w["kv_norm"], w["w_uk"], w["w_uv"], te)


def _softmax_pv(s_list, v_list):
    m = s_list[0].max(axis=-1, keepdims=True)
    for s in s_list[1:]:
        m = jnp.maximum(m, s.max(axis=-1, keepdims=True))
    den = None
    out = None
    for s, v in zip(s_list, v_list):
        p = jnp.exp2(s - m)
        l = p.sum(axis=-1, keepdims=True)
        o = _dot(p.astype(BF16), v)
        den = l if den is None else den + l
        out = o if out is None else out + o
    return out / den


def _na_kernel(vid_ref, st_ref, q_ref, k_ref, v_ref, km_ref, vm_ref, *rest):
    del vid_ref
    bias_refs, (o_ref, kcat, vcat) = rest[:NA_UNROLL], rest[NA_UNROLL:]
    i = pl.program_id(1)
    low = lax.broadcasted_iota(jnp.int32, (NA_QB, LANES), 1) < NA_HEAD_DIM
    zero_rows = jnp.zeros((NA_KCAT - NA_KB - N_META, NA_DIM), BF16)
    for u in range(NA_UNROLL):
        start = pl.multiple_of(st_ref[i * NA_UNROLL + u] * GRID_W, GRID_W)
        for cat, win, meta in ((kcat, k_ref, km_ref), (vcat, v_ref, vm_ref)):
            cat[u, 0:NA_KB, :] = win[pl.ds(start, NA_KB), :]
            cat[u, NA_KB:NA_KB + N_META, :] = meta[...]
            cat[u, NA_KB + N_META:, :] = zero_rows
    chains = [(u, j) for u in range(NA_UNROLL) for j in range(NA_HEADS // 2)]
    sl = lambda j: slice(j * LANES, (j + 1) * LANES)
    rows = lambda u: slice(u * NA_QB, (u + 1) * NA_QB)

    def scores(u, j):
        qp = q_ref[rows(u), sl(j)]
        q2 = jnp.concatenate([jnp.where(low, qp, jnp.zeros_like(qp)),
                              jnp.where(low, jnp.zeros_like(qp), qp)], axis=0)
        return _dot_nt(q2, kcat[u, :, sl(j)]) + bias_refs[u][j]

    s_next = scores(*chains[0])
    for n, (u, j) in enumerate(chains):
        s = s_next
        if n + 1 < len(chains):
            s_next = scores(*chains[n + 1])
        parts = [s[:, t * LANES:(t + 1) * LANES] for t in range(NA_KCAT // LANES)]
        m = functools.reduce(jnp.maximum, parts).max(axis=-1, keepdims=True)
        ps = [jnp.exp2(part - m) for part in parts]
        den = functools.reduce(jnp.add, ps).sum(axis=-1, keepdims=True)
        o = _dot(jnp.concatenate([p.astype(BF16) for p in ps], axis=-1), vcat[u, :, sl(j)]) / den
        o_ref[rows(u), sl(j)] = jnp.where(low, o[:NA_QB], o[NA_QB:]).astype(BF16)


def _na(nq, nk, nv, nkm, nvm, bias, layer, vid, st, batch, seq, meta_off):
    nstep = seq // (NA_QB * NA_UNROLL)
    n_patterns = bias.shape[0] // DEPTH
    qspec = pl.BlockSpec((NA_QB * NA_UNROLL, NA_DIM), lambda b, i, vid, st: (b * nstep + i, 0))
    kvspec = pl.BlockSpec((seq, NA_DIM), lambda b, i, vid, st: (b, 0))
    mspec = pl.BlockSpec((N_META, NA_DIM), lambda b, i, vid, st: (meta_off + b, 0))
    bias_spec = lambda u: pl.BlockSpec(
        (None, NA_HEADS // 2, 2 * NA_QB, NA_KCAT),
        lambda b, i, vid, st: (layer * n_patterns + vid[i * NA_UNROLL + u], 0, 0, 0))
    grid_spec = pltpu.PrefetchScalarGridSpec(
        num_scalar_prefetch=2,
        grid=(batch, nstep),
        in_specs=[qspec, kvspec, kvspec, mspec, mspec] + [bias_spec(u) for u in range(NA_UNROLL)],
        out_specs=qspec,
        scratch_shapes=[pltpu.VMEM((NA_UNROLL, NA_KCAT, NA_DIM), BF16)] * 2)
    return pl.pallas_call(
        _na_kernel,
        out_shape=jax.ShapeDtypeStruct((batch * seq, NA_DIM), BF16),
        grid_spec=grid_spec,
        compiler_params=_params(2),
        name="na",
    )(vid, st, nq, nk, nv, nkm, nvm, *([bias] * NA_UNROLL))


def _na_meta_kernel(q_ref, k_ref, v_ref, mb_ref, o_ref):
    low = lax.broadcasted_iota(jnp.int32, (N_META, LANES), 1) < NA_HEAD_DIM
    for j in range(NA_HEADS // 2):
        sl = slice(j * LANES, (j + 1) * LANES)
        qp = q_ref[:, sl]
        halves = []
        for half in range(2):
            h = 2 * j + half
            qh = jnp.where(low if half == 0 else jnp.logical_not(low), qp, jnp.zeros_like(qp))
            s = _dot_nt(qh, k_ref[:, sl]) + mb_ref[h:h + 1, :] * LOG2_E
            halves.append(_softmax_pv([s], [v_ref[:, sl]]))
        o_ref[:, sl] = jnp.where(low, halves[0], halves[1]).astype(BF16)


def _na_meta(nqm, nkm, nvm, mb, layer):
    n = nqm.shape[0]
    spec = pl.BlockSpec((N_META, NA_DIM), lambda b: (b, 0))
    return pl.pallas_call(
        _na_meta_kernel,
        out_shape=jax.ShapeDtypeStruct((n, NA_DIM), BF16),
        grid=(n // N_META,),
        in_specs=[spec, spec, spec, pl.BlockSpec((None, NA_HEADS, N_META), lambda b: (layer, 0, 0))],
        out_specs=spec,
        compiler_params=_params(1),
        name="na_meta",
    )(nqm, nkm, nvm, mb)


_MLA_EXP2_SCALE = (MLA_NOPE_DIM + MLA_ROPE_DIM) ** -0.5 * math.log2(math.e)


def _mla_kernel(q_ref, k_ref, v_ref, km_ref, vm_ref, o_ref, m_scr, l_scr, acc_scr, *, seq):
    tq = q_ref.shape[0]
    hsl = lambda h: slice(h * HEAD_SLOT, (h + 1) * HEAD_SLOT)
    vsl = lambda h: slice((h // 2) * LANES, (h // 2 + 1) * LANES)
    tiles = lambda x: [x[:, t * LANES:(t + 1) * LANES] for t in range(x.shape[1] // LANES)]

    def softmax_piece(s, m_old):
        parts = tiles(s)
        lane_max = functools.reduce(jnp.maximum, parts)
        mn = jnp.broadcast_to(lane_max.max(axis=-1, keepdims=True), (tq, LANES))
        if m_old is not None:
            mn = jnp.maximum(m_old, mn)
        ps = [jnp.exp2(part - mn) for part in parts]
        return mn, jnp.concatenate([p.astype(BF16) for p in ps], axis=-1), functools.reduce(jnp.add, ps)

    col = lax.broadcasted_iota(jnp.int32, (tq, LANES), 1)
    pad_mask = jnp.where(col < N_META, 0.0, MASK_VALUE).astype(F32)

    def chunk(off, first):
        def scores(h):
            s = _dot_nt(q_ref[:, hsl(h)], k_ref[pl.ds(off, MLA_CK), hsl(h)])
            if first:
                s = jnp.concatenate([s, _dot_nt(q_ref[:, hsl(h)], km_ref[:, hsl(h)]) + pad_mask], axis=-1)
            return s

        def values(p, h):
            o = _dot(p[:, :MLA_CK], v_ref[pl.ds(off, MLA_CK), vsl(h)])
            return o + _dot(p[:, MLA_CK:], vm_ref[:, vsl(h)]) if first else o

        ahead = [scores(h) for h in range(MLA_AHEAD)]
        for h in range(MLA_HEADS):
            s = ahead.pop(0)
            if h + MLA_AHEAD < MLA_HEADS:
                ahead.append(scores(h + MLA_AHEAD))
            if first:
                m_scr[h], p, l_scr[h] = softmax_piece(s, None)
                acc_scr[h] = values(p, h)
            else:
                m = m_scr[h]
                mn, p, lsum = softmax_piece(s, m)
                alpha = jnp.exp2(m - mn)
                m_scr[h] = mn
                l_scr[h] = alpha * l_scr[h] + lsum
                acc_scr[h] = alpha * acc_scr[h] + values(p, h)

    for ci in range(seq // MLA_CK):
        chunk(ci * MLA_CK, ci == 0)
    low = col < MLA_V_DIM
    out = [acc_scr[h] / l_scr[h].sum(axis=-1, keepdims=True) for h in range(MLA_HEADS)]
    for j in range(MLA_HEADS // 2):
        o_ref[:, vsl(2 * j)] = jnp.where(low, out[2 * j], out[2 * j + 1]).astype(BF16)


def _mla(q, k, v, km, vm, batch, seq, meta_off, tq, q_rows_per_batch, q_off):
    nq = q_rows_per_batch // tq
    qspec = lambda c: pl.BlockSpec((tq, c), lambda b, i: (q_off + b * nq + i, 0))
    kvspec = lambda c: pl.BlockSpec((seq, c), lambda b, i: (b, 0))
    mspec = lambda c: pl.BlockSpec((LANES, c), lambda b, i: (meta_off + b, 0))
    ospec = pl.BlockSpec((tq, MLA_DIM), lambda b, i: (b * nq + i, 0))
    return pl.pallas_call(
        functools.partial(_mla_kernel, seq=seq),
        out_shape=jax.ShapeDtypeStruct((batch * q_rows_per_batch, MLA_DIM), BF16),
        grid=(batch, nq),
        in_specs=[qspec(MLA_HEADS * HEAD_SLOT), kvspec(MLA_HEADS * HEAD_SLOT), kvspec(MLA_DIM),
                  mspec(MLA_HEADS * HEAD_SLOT), mspec(MLA_DIM)],
        out_specs=ospec,
        scratch_shapes=[pltpu.VMEM((MLA_HEADS, tq, LANES), F32)] * 3,
        compiler_params=_params(2),
        name="mla",
    )(q, k, v, km, vm)


def _mix_ffn_kernel(h_ref, cb_ref, ccu_ref, halo_a_ref, halo_b_ref, halo_c_ref, yna_ref, ymla_ref,
                    convw_ref, cn_ref, nn_ref, mn_ref, wo_ref, post_ref,
                    fpre_ref, fwgu_ref, fwd_ref, fpost_ref, o_ref, *, meta, tiles_per_seq, prompt_batches):
    tm = h_ref.shape[0]
    i = pl.program_id(0)
    ccu = ccu_ref[...].astype(F32)
    first_row = lambda ref: ref[...].astype(F32)[0:1, :]
    last_row = lambda ref: ref[...].astype(F32)[HALO_ROWS - 1:HALO_ROWS, :]
    if meta:
        prev_row = jnp.zeros((1, CONV_DIM), F32)
        next_row = jnp.where(i < prompt_batches, first_row(halo_a_ref), first_row(halo_b_ref))
    else:
        t = i % tiles_per_seq
        prev_row = jnp.where(t == 0, last_row(halo_c_ref), last_row(halo_a_ref))
        next_row = jnp.where(t == tiles_per_seq - 1, jnp.zeros((1, CONV_DIM), F32), first_row(halo_b_ref))
    row = lax.broadcasted_iota(jnp.int32, (tm, CONV_DIM), 0)
    dn = jnp.where(row == 0, prev_row, pltpu.roll(ccu, 1, axis=0))
    up = jnp.where(row == tm - 1, next_row, pltpu.roll(ccu, tm - 1, axis=0))
    w = convw_ref[...]
    yc = cb_ref[...].astype(F32) * (w[0:1, :] * dn + w[1:2, :] * ccu + w[2:3, :] * up)
    y = jnp.concatenate([_rms(yc, cn_ref[...]), _rms(yna_ref[...].astype(F32), nn_ref[...]),
                         _rms(ymla_ref[...].astype(F32), mn_ref[...])], axis=-1).astype(BF16)
    x = h_ref[...] + _rms(_dot(y, wo_ref[...]), post_ref[...])
    o_ref[...] = _ffn_half_step(x, fpre_ref, fwgu_ref, fwd_ref, fpost_ref)


def _mix_ffn(h, cb, ccu, halo_a, halo_b, halo_c, yna, ymla, layer, w, *, meta, tm, seq=None,
         meta_off=0, prompt_batches=0, seq_a=0, seq_b=0):
    n = h.shape[0]
    row = lambda c: pl.BlockSpec((tm, c), lambda i: (i, 0))
    halo = lambda index_map: pl.BlockSpec((HALO_ROWS, CONV_DIM), index_map)
    if meta:
        tiles_per_seq = 1
        a_spec = halo(lambda i: (jnp.minimum(i, prompt_batches - 1) * (seq_a // HALO_ROWS), 0))
        b_spec = halo(lambda i: (jnp.maximum(i - prompt_batches, 0) * (seq_b // HALO_ROWS), 0))
        c_spec = halo(lambda i: (0, 0))
    else:
        tiles_per_seq = seq // tm
        last = n // HALO_ROWS - 1
        a_spec = halo(lambda i: (jnp.maximum(i * (tm // HALO_ROWS) - 1, 0), 0))
        b_spec = halo(lambda i: (jnp.minimum((i + 1) * (tm // HALO_ROWS), last), 0))
        c_spec = halo(lambda i: (meta_off + i // tiles_per_seq, 0))
    return pl.pallas_call(
        functools.partial(_mix_ffn_kernel, meta=meta, tiles_per_seq=tiles_per_seq, prompt_batches=prompt_batches),
        out_shape=jax.ShapeDtypeStruct(h.shape, F32),
        grid=(n // tm,),
        in_specs=[row(D_MODEL), row(CONV_DIM), row(CONV_DIM), a_spec, b_spec, c_spec,
                  row(NA_DIM), row(MLA_DIM),
                  _const_spec((None, 3, CONV_DIM), (layer, 0, 0)),
                  _const_spec((None, 1, CONV_DIM), (layer, 0, 0)),
                  _const_spec((None, 1, NA_DIM), (layer, 0, 0)),
                  _const_spec((None, 1, MLA_DIM), (layer, 0, 0)),
                  _const_spec((None, D_MODEL, D_MODEL), (layer, 0, 0)),
                  _const_spec((None, 1, D_MODEL), (layer, 0, 0))] + _ffn_specs(layer),
        out_specs=row(D_MODEL),
        compiler_params=_params(1),
        name="mix_ffn",
    )(h, cb, ccu, halo_a, halo_b, halo_c, yna, ymla,
      w["conv_w"], w["conv_on"], w["na_on"], w["mla_on"], w["w_o"], w["mix_post"], *w["ffn2"])


def _na_patterns():
    patterns = []
    per_rows = {}
    for rows in (2048 // GRID_W, 4096 // GRID_W):
        win_h = min(NA_MAX_WIN_H, rows)
        vids, starts = [], []
        for blk in range(rows // NA_QROWS):
            r0 = blk * NA_QROWS
            start = int(np.clip(r0 - win_h // 2, 0, rows - NA_KROWS))
            qr = r0 + np.arange(NA_QROWS)[:, None]
            kr = start + np.arange(NA_KROWS)[None, :]
            rs = np.clip(qr - win_h // 2, 0, rows - win_h)
            inside = (kr >= rs) & (kr < rs + win_h)
            assert (inside.sum(axis=1) == win_h).all()
            rel = np.where(inside, kr - qr + NA_MAX_WIN_H - 1, -1)
            for n, p in enumerate(patterns):
                if np.array_equal(p, rel):
                    vids.append(n)
                    break
            else:
                vids.append(len(patterns))
                patterns.append(rel)
            starts.append(start)
        per_rows[rows] = (np.asarray(vids, np.int32), np.asarray(starts, np.int32))
    return np.stack(patterns), per_rows


_NA_REL_ROWS, _NA_BLOCKS = _na_patterns()


def _na_col_tables():
    qc = np.arange(GRID_W)[:, None]
    kc = np.arange(GRID_W)[None, :]
    cs = np.clip(qc - NA_WIN_W // 2, 0, GRID_W - NA_WIN_W)
    inside = (kc >= cs) & (kc < cs + NA_WIN_W)
    rel = kc - qc + NA_WIN_W - 1
    onehot = (rel[None] == np.arange(2 * NA_WIN_W - 1)[:, None, None]) & inside[None]
    return onehot.astype(np.float32), inside


_NA_COL_ONEHOT, _NA_COL_INSIDE = _na_col_tables()


def _na_bias_tables(rpb, meta_bias):
    blocks = jnp.einsum("lhrd,dqk->lhrqk", rpb.astype(F32), _NA_COL_ONEHOT, precision=lax.Precision.HIGHEST)
    blocks = jnp.where(_NA_COL_INSIDE, blocks * LOG2_E, MASK_VALUE)
    masked = jnp.full(blocks.shape[:2] + (GRID_W, GRID_W), MASK_VALUE, F32)
    meta_cols = jnp.broadcast_to((meta_bias.astype(F32) * LOG2_E)[:, :, None, :],
                                 meta_bias.shape[:2] + (GRID_W, N_META))
    pad_cols = jnp.full(meta_bias.shape[:2] + (GRID_W, NA_KCAT - NA_KB - N_META), MASK_VALUE, F32)
    tables = []
    for rel in _NA_REL_ROWS:
        rows = [jnp.concatenate([masked if r < 0 else blocks[:, :, int(r)] for r in rel_q] + [meta_cols, pad_cols],
                                axis=-1) for rel_q in rel]
        tables.append(jnp.concatenate(rows, axis=-2))
    return jnp.stack(tables, axis=1).reshape(-1, NA_HEADS // 2, 2 * NA_QB, NA_KCAT)


def _rope_table(pos):
    half = MLA_ROPE_DIM // 2
    inv_freq = ROPE_THETA ** (-jnp.arange(half, dtype=F32) / half)
    ang = pos.astype(F32)[:, None] * inv_freq[None, :]
    cos = jnp.concatenate([jnp.cos(ang)] * 2, axis=-1)
    sin = jnp.concatenate([jnp.sin(ang)] * 2, axis=-1)
    return jnp.concatenate([cos, sin, cos, sin], axis=-1)


def _rot_cols(w):
    half = MLA_ROPE_DIM // 2
    return jnp.concatenate([-w[..., half:], w[..., :half]], axis=-1)


def _prep_weights(ffn1_pre, ffn1_gu, ffn1_down, ffn1_post, mix_pre, w_in, conv_w, na_rpb, na_mb, q_norm, w_uq,
                  kv_norm, w_ukv, conv_on, na_on, mla_on, w_o, mix_post, ffn2_pre, ffn2_gu, ffn2_down, ffn2_post):
    vec = lambda g: g[:, None, :]
    kpe = w_in[:, :, 2560:2560 + MLA_ROPE_DIM]
    kpe_rot = _rot_cols(kpe)
    w_in_ext = jnp.concatenate([w_in[:, :, :2560], kpe, kpe_rot, kpe, kpe_rot], axis=-1).astype(BF16)
    per_head = w_uq.reshape(DEPTH, MLA_Q_RANK, MLA_HEADS, MLA_NOPE_DIM + MLA_ROPE_DIM)
    nope, pe = per_head[..., :MLA_NOPE_DIM], per_head[..., MLA_NOPE_DIM:]
    even = jnp.concatenate([nope, pe, _rot_cols(pe)], axis=-1)
    odd = jnp.concatenate([pe, _rot_cols(pe), nope], axis=-1)
    is_even = (jnp.arange(MLA_HEADS) % 2 == 0)[None, None, :, None]
    w_uq_ext = jnp.where(is_even, even, odd).reshape(DEPTH, MLA_Q_RANK, MLA_HEADS * HEAD_SLOT).astype(BF16)
    kv_heads = w_ukv.reshape(DEPTH, MLA_KV_RANK, MLA_HEADS, MLA_NOPE_DIM + MLA_V_DIM)
    w_uk = kv_heads[..., :MLA_NOPE_DIM].reshape(DEPTH, MLA_KV_RANK, MLA_HEADS * MLA_NOPE_DIM).astype(BF16)
    w_uv = kv_heads[..., MLA_NOPE_DIM:].reshape(DEPTH, MLA_KV_RANK, MLA_DIM).astype(BF16)
    bias = _na_bias_tables(na_rpb, na_mb)
    return dict(
        ffn1=(vec(ffn1_pre), ffn1_gu.astype(BF16), ffn1_down.astype(BF16), vec(ffn1_post)),
        ffn2=(vec(ffn2_pre), ffn2_gu.astype(BF16), ffn2_down.astype(BF16), vec(ffn2_post)),
        mix_pre=vec(mix_pre), w_in=w_in_ext, q_norm=vec(q_norm), w_uq=w_uq_ext, kv_norm=vec(kv_norm),
        w_uk=w_uk, w_uv=w_uv, conv_w=conv_w, conv_on=vec(conv_on), na_on=vec(na_on), mla_on=vec(mla_on),
        w_o=w_o.astype(BF16), mix_post=vec(mix_post), na_bias=bias, na_mb=na_mb)


def _pad_meta_rows(x):
    c = x.shape[-1]
    x = x.reshape(-1, N_META, c)
    return jnp.pad(x, ((0, 0), (0, LANES - N_META), (0, 0))).reshape(-1, c)


def kernel(x_prompt, x_sample, meta_tokens, ffn1_pre_norm, ffn1_w_gu, ffn1_w_down, ffn1_post_norm, mix_pre_norm,
           w_in, conv_w, na_rpb, na_meta_bias, mla_q_norm, mla_w_uq, mla_kv_norm, mla_w_ukv, conv_out_norm,
           na_out_norm, mla_out_norm, w_o, mix_post_norm, ffn2_pre_norm, ffn2_w_gu, ffn2_w_down, ffn2_post_norm):
    w = _prep_weights(ffn1_pre_norm, ffn1_w_gu, ffn1_w_down, ffn1_post_norm, mix_pre_norm, w_in, conv_w, na_rpb,
                      na_meta_bias, mla_q_norm, mla_w_uq, mla_kv_norm, mla_w_ukv, conv_out_norm, na_out_norm,
                      mla_out_norm, w_o, mix_post_norm, ffn2_pre_norm, ffn2_w_gu, ffn2_w_down, ffn2_post_norm)
    groups = []
    for x in (x_prompt, x_sample):
        b, t, _ = x.shape
        groups.append(dict(batch=b, seq=t, te=_rope_table(N_META + jnp.arange(t))))
    nb = [g["batch"] for g in groups]
    meta_off = [0, nb[0]]
    n_meta_rows = sum(nb) * N_META
    te_meta = jnp.tile(_rope_table(jnp.arange(N_META)), (sum(nb), 1))
    hs = [x_prompt.reshape(-1, D_MODEL), x_sample.reshape(-1, D_MODEL),
          jnp.tile(meta_tokens.astype(F32), (sum(nb), 1))]

    for layer in range(DEPTH):
        proj = [_ffn_inproj(hs[g], layer, w, groups[g]["te"], groups[g]["seq"] // ROW_TILE, ROW_TILE)
                for g in range(2)]
        proj.append(_ffn_inproj(hs[2], layer, w, te_meta, 1, n_meta_rows))
        hs = [p[0] for p in proj]
        cb_m, ccu_m, nq_m, nk_m, nv_m, q_m, k_m, v_m = proj[2][1:]
        k_mp, v_mp = _pad_meta_rows(k_m), _pad_meta_rows(v_m)
        meta_live = layer < DEPTH - 1
        new_hs, ymla_m = [], []
        for g in range(2):
            cb, ccu, nq, nk, nv, q, k, v = proj[g][1:]
            b, t = groups[g]["batch"], groups[g]["seq"]
            vid, st = _NA_BLOCKS[t // GRID_W]
            yna = _na(nq, nk, nv, nk_m, nv_m, w["na_bias"], layer,
                      jnp.asarray(vid), jnp.asarray(st), b, t, meta_off[g])
            ymla = _mla(q, k, v, k_mp, v_mp, b, t, meta_off[g], MLA_TQ, t, 0)
            if meta_live:
                ymla_m.append(_mla(q_m, k, v, k_mp, v_mp, b, t, meta_off[g], N_META, N_META, meta_off[g]))
            new_hs.append(_mix_ffn(hs[g], cb, ccu, ccu, ccu, ccu_m, yna, ymla, layer, w, meta=False, tm=ROW_TILE,
                                   seq=t, meta_off=meta_off[g]))
        if meta_live:
            yna_m = _na_meta(nq_m, nk_m, nv_m, w["na_mb"], layer)
            new_hs.append(_mix_ffn(hs[2], cb_m, ccu_m, proj[0][2], proj[1][2], ccu_m, yna_m,
                                   jnp.concatenate(ymla_m, axis=0), layer, w, meta=True, tm=N_META,
                                   prompt_batches=nb[0], seq_a=groups[0]["seq"], seq_b=groups[1]["seq"]))
        hs = new_hs
    return (hs[0].reshape(x_prompt.shape), hs[1].reshape(x_sample.shape))
```

```python
import functools
import math

import numpy as np
import jax
import jax.numpy as jnp
from jax import lax
from jax.experimental import pallas as pl
from jax.experimental.pallas import tpu as pltpu

F32 = jnp.float32
BF16 = jnp.bfloat16

D_MODEL = 1024
DEPTH = 4
N_META = 16
GRID_W = 64
CONV_DIM = 256
NA_HEADS = 4
NA_HEAD_DIM = 64
NA_DIM = NA_HEADS * NA_HEAD_DIM
NA_MAX_WIN_H = 8
NA_WIN_W = 16
MLA_HEADS = 8
MLA_NOPE_DIM = 64
MLA_ROPE_DIM = 32
MLA_V_DIM = 64
MLA_Q_RANK = 768
MLA_KV_RANK = 256
MLA_DIM = MLA_HEADS * MLA_V_DIM
FFN_DIM = 2816
ROPE_THETA = 10000.0
RMS_EPS = 1e-6

LANES = 128
HEAD_SLOT = 128
IN_COLS = 6 * 256 + MLA_Q_RANK + MLA_KV_RANK + LANES
FFN_CHUNK = 256
ROW_TILE = 512
HALO_ROWS = 16
NA_QROWS = 2
NA_KROWS = NA_MAX_WIN_H + NA_QROWS - 1
NA_QB = NA_QROWS * GRID_W
NA_KB = NA_KROWS * GRID_W
NA_KCAT = 640
NA_UNROLL = 8
MLA_TQ = 256
MLA_CK = 1024
MLA_AHEAD = 1
MASK_VALUE = -1e30
LOG2_E = math.log2(math.e)
VMEM_LIMIT = 56 * 1024 * 1024


def _rms(x, g):
    ms = jnp.mean(x * x, axis=-1, keepdims=True)
    return x * lax.rsqrt(ms + RMS_EPS) * g


def _dot(a, b):
    return jnp.dot(a, b, preferred_element_type=F32)


def _dot_nt(a, b):
    return lax.dot_general(a, b, (((1,), (1,)), ((), ())), preferred_element_type=F32)


def _const_spec(block_shape, index):
    return pl.BlockSpec(block_shape, lambda *_: index, pipeline_mode=pl.Buffered(1))


def _params(n_axes, flags=None):
    return pltpu.CompilerParams(dimension_semantics=("arbitrary",) * n_axes,
                                vmem_limit_bytes=VMEM_LIMIT, flags=flags)


def _ffn_half_step(x, pre_ref, wgu_ref, wd_ref, post_ref):
    xn = _rms(x, pre_ref[...]).astype(BF16)
    acc = None
    for c in range(FFN_DIM // FFN_CHUNK):
        lo = c * FFN_CHUNK
        g = _dot(xn, wgu_ref[:, lo:lo + FFN_CHUNK])
        u = _dot(xn, wgu_ref[:, FFN_DIM + lo:FFN_DIM + lo + FFN_CHUNK])
        a = (g * jax.nn.sigmoid(g) * u).astype(BF16)
        d = _dot(a, wd_ref[lo:lo + FFN_CHUNK, :])
        acc = d if acc is None else acc + d
    return x + 0.5 * _rms(acc, post_ref[...])


def _ffn_specs(layer):
    return [_const_spec((None, 1, D_MODEL), (layer, 0, 0)),
            _const_spec((None, D_MODEL, 2 * FFN_DIM), (layer, 0, 0)),
            _const_spec((None, FFN_DIM, D_MODEL), (layer, 0, 0)),
            _const_spec((None, 1, D_MODEL), (layer, 0, 0))]


def _ffn_inproj_kernel(h_ref, fpre_ref, fwgu_ref, fwd_ref, fpost_ref,
                       pre_ref, win_ref, qn_ref, wuq_ref, kvn_ref, wuk_ref, wuv_ref, te_ref,
                       h_out_ref, cb_ref, ccu_ref, nq_ref, nk_ref, nv_ref, q_ref, k_ref, v_ref):
    tm = h_ref.shape[0]
    x = _ffn_half_step(h_ref[...], fpre_ref, fwgu_ref, fwd_ref, fpost_ref)
    h_out_ref[...] = x
    xn = _rms(x, pre_ref[...]).astype(BF16)
    z = _dot(xn, win_ref[...])
    cb_ref[...] = z[:, 0:256].astype(BF16)
    ccu_ref[...] = (z[:, 256:512] * z[:, 512:768]).astype(BF16)
    nq_ref[...] = (z[:, 768:1024] * (NA_HEAD_DIM ** -0.5 * LOG2_E)).astype(BF16)
    nk_ref[...] = z[:, 1024:1280].astype(BF16)
    nv_ref[...] = z[:, 1280:1536].astype(BF16)
    ql = _rms(z[:, 1536:1536 + MLA_Q_RANK], qn_ref[...]).astype(BF16)
    qf = _dot(ql, wuq_ref[...])
    kvl = _rms(z[:, 2304:2304 + MLA_KV_RANK], kvn_ref[...]).astype(BF16)
    kn = _dot(kvl, wuk_ref[...])
    v_ref[...] = _dot(kvl, wuv_ref[...]).astype(BF16)
    te = te_ref[...]
    y = z[:, 2560:2560 + LANES] * te
    kr = y + pltpu.roll(y, MLA_ROPE_DIM, axis=1)
    low = lax.broadcasted_iota(jnp.int32, (tm, LANES), 1) < MLA_NOPE_DIM
    for h in range(MLA_HEADS):
        sl = slice(h * HEAD_SLOT, (h + 1) * HEAD_SLOT)
        qh = qf[:, sl] * _MLA_EXP2_SCALE
        qt = qh * te
        q_ref[:, sl] = (jnp.where(low, qh, qt) if h % 2 == 0 else jnp.where(low, qt, qh)).astype(BF16)
    for j in range(MLA_HEADS // 2):
        kp = kn[:, j * LANES:(j + 1) * LANES]
        k_ref[:, (2 * j) * HEAD_SLOT:(2 * j + 1) * HEAD_SLOT] = jnp.where(low, kp, kr).astype(BF16)
        k_ref[:, (2 * j + 1) * HEAD_SLOT:(2 * j + 2) * HEAD_SLOT] = jnp.where(low, kr, kp).astype(BF16)


def _ffn_inproj(h, layer, w, te, te_blocks, tm):
    n = h.shape[0]
    row = lambda c: pl.BlockSpec((tm, c), lambda i: (i, 0))
    outs = [(D_MODEL, F32), (256, BF16), (256, BF16), (256, BF16), (256, BF16), (256, BF16),
            (MLA_HEADS * HEAD_SLOT, BF16), (MLA_HEADS * HEAD_SLOT, BF16), (MLA_DIM, BF16)]
    return pl.pallas_call(
        _ffn_inproj_kernel,
        out_shape=[jax.ShapeDtypeStruct((n, c), dt) for c, dt in outs],
        grid=(n // tm,),
        in_specs=[row(D_MODEL)] + _ffn_specs(layer) + [
                  _const_spec((None, 1, D_MODEL), (layer, 0, 0)),
                  _const_spec((None, D_MODEL, IN_COLS), (layer, 0, 0)),
                  _const_spec((None, 1, MLA_Q_RANK), (layer, 0, 0)),
                  _const_spec((None, MLA_Q_RANK, MLA_HEADS * HEAD_SLOT), (layer, 0, 0)),
                  _const_spec((None, 1, MLA_KV_RANK), (layer, 0, 0)),
                  _const_spec((None, MLA_KV_RANK, MLA_HEADS * MLA_NOPE_DIM), (layer, 0, 0)),
                  _const_spec((None, MLA_KV_RANK, MLA_DIM), (layer, 0, 0)),
                  pl.BlockSpec((tm, LANES), lambda i: (i % te_blocks, 0))],
        out_specs=[row(c) for c, _ in outs],
        compiler_params=_params(1),
        name="ffn_inproj",
    )(h, *w["ffn1"], w["mix_pre"], w["w_in"], w["q_norm"], w["w_uq"], w["kv_norm"], w["w_uk"], w["w_uv"], te)


def _softmax_pv(s_list, v_list):
    m = s_list[0].max(axis=-1, keepdims=True)
    for s in s_list[1:]:
        m = jnp.maximum(m, s.max(axis=-1, keepdims=True))
    den = None
    out = None
    for s, v in zip(s_list, v_list):
        p = jnp.exp2(s - m)
        l = p.sum(axis=-1, keepdims=True)
        o = _dot(p.astype(BF16), v)
        den = l if den is None else den + l
        out = o if out is None else out + o
    return out / den


def _na_kernel(vid_ref, st_ref, q_ref, k_ref, v_ref, km_ref, vm_ref, *rest):
    del vid_ref
    bias_refs, (o_ref, kcat, vcat) = rest[:NA_UNROLL], rest[NA_UNROLL:]
    i = pl.program_id(1)
    low = lax.broadcasted_iota(jnp.int32, (NA_QB, LANES), 1) < NA_HEAD_DIM
    zero_rows = jnp.zeros((NA_KCAT - NA_KB - N_META, NA_DIM), BF16)
    for u in range(NA_UNROLL):
        start = pl.multiple_of(st_ref[i * NA_UNROLL + u] * GRID_W, GRID_W)
        for cat, win, meta in ((kcat, k_ref, km_ref), (vcat, v_ref, vm_ref)):
            cat[u, 0:NA_KB, :] = win[pl.ds(start, NA_KB), :]
            cat[u, NA_KB:NA_KB + N_META, :] = meta[...]
            cat[u, NA_KB + N_META:, :] = zero_rows
    chains = [(u, j) for u in range(NA_UNROLL) for j in range(NA_HEADS // 2)]
    sl = lambda j: slice(j * LANES, (j + 1) * LANES)
    rows = lambda u: slice(u * NA_QB, (u + 1) * NA_QB)

    def scores(u, j):
        qp = q_ref[rows(u), sl(j)]
        q2 = jnp.concatenate([jnp.where(low, qp, jnp.zeros_like(qp)),
                              jnp.where(low, jnp.zeros_like(qp), qp)], axis=0)
        return _dot_nt(q2, kcat[u, :, sl(j)]) + bias_refs[u][j]

    s_next = scores(*chains[0])
    for n, (u, j) in enumerate(chains):
        s = s_next
        if n + 1 < len(chains):
            s_next = scores(*chains[n + 1])
        parts = [s[:, t * LANES:(t + 1) * LANES] for t in range(NA_KCAT // LANES)]
        m = functools.reduce(jnp.maximum, parts).max(axis=-1, keepdims=True)
        ps = [jnp.exp2(part - m) for part in parts]
        den = functools.reduce(jnp.add, ps).sum(axis=-1, keepdims=True)
        o = _dot(jnp.concatenate([p.astype(BF16) for p in ps], axis=-1), vcat[u, :, sl(j)]) / den
        o_ref[rows(u), sl(j)] = jnp.where(low, o[:NA_QB], o[NA_QB:]).astype(BF16)


def _na(nq, nk, nv, nkm, nvm, bias, layer, vid, st, batch, seq, meta_off):
    nstep = seq // (NA_QB * NA_UNROLL)
    n_patterns = bias.shape[0] // DEPTH
    qspec = pl.BlockSpec((NA_QB * NA_UNROLL, NA_DIM), lambda b, i, vid, st: (b * nstep + i, 0))
    kvspec = pl.BlockSpec((seq, NA_DIM), lambda b, i, vid, st: (b, 0))
    mspec = pl.BlockSpec((N_META, NA_DIM), lambda b, i, vid, st: (meta_off + b, 0))
    bias_spec = lambda u: pl.BlockSpec(
        (None, NA_HEADS // 2, 2 * NA_QB, NA_KCAT),
        lambda b, i, vid, st: (layer * n_patterns + vid[i * NA_UNROLL + u], 0, 0, 0))
    grid_spec = pltpu.PrefetchScalarGridSpec(
        num_scalar_prefetch=2,
        grid=(batch, nstep),
        in_specs=[qspec, kvspec, kvspec, mspec, mspec] + [bias_spec(u) for u in range(NA_UNROLL)],
        out_specs=qspec,
        scratch_shapes=[pltpu.VMEM((NA_UNROLL, NA_KCAT, NA_DIM), BF16)] * 2)
    return pl.pallas_call(
        _na_kernel,
        out_shape=jax.ShapeDtypeStruct((batch * seq, NA_DIM), BF16),
        grid_spec=grid_spec,
        compiler_params=_params(2),
        name="na",
    )(vid, st, nq, nk, nv, nkm, nvm, *([bias] * NA_UNROLL))


def _na_meta_kernel(q_ref, k_ref, v_ref, mb_ref, o_ref):
    low = lax.broadcasted_iota(jnp.int32, (N_META, LANES), 1) < NA_HEAD_DIM
    for j in range(NA_HEADS // 2):
        sl = slice(j * LANES, (j + 1) * LANES)
        qp = q_ref[:, sl]
        halves = []
        for half in range(2):
            h = 2 * j + half
            qh = jnp.where(low if half == 0 else jnp.logical_not(low), qp, jnp.zeros_like(qp))
            s = _dot_nt(qh, k_ref[:, sl]) + mb_ref[h:h + 1, :] * LOG2_E
            halves.append(_softmax_pv([s], [v_ref[:, sl]]))
        o_ref[:, sl] = jnp.where(low, halves[0], halves[1]).astype(BF16)


def _na_meta(nqm, nkm, nvm, mb, layer):
    n = nqm.shape[0]
    spec = pl.BlockSpec((N_META, NA_DIM), lambda b: (b, 0))
    return pl.pallas_call(
        _na_meta_kernel,
        out_shape=jax.ShapeDtypeStruct((n, NA_DIM), BF16),
        grid=(n // N_META,),
        in_specs=[spec, spec, spec, pl.BlockSpec((None, NA_HEADS, N_META), lambda b: (layer, 0, 0))],
        out_specs=spec,
        compiler_params=_params(1),
        name="na_meta",
    )(nqm, nkm, nvm, mb)


_MLA_EXP2_SCALE = (MLA_NOPE_DIM + MLA_ROPE_DIM) ** -0.5 * math.log2(math.e)


def _mla_kernel(q_ref, k_ref, v_ref, km_ref, vm_ref, o_ref, m_scr, l_scr, acc_scr, *, seq):
    tq = q_ref.shape[0]
    hsl = lambda h: slice(h * HEAD_SLOT, (h + 1) * HEAD_SLOT)
    vsl = lambda h: slice((h // 2) * LANES, (h // 2 + 1) * LANES)
    tiles = lambda x: [x[:, t * LANES:(t + 1) * LANES] for t in range(x.shape[1] // LANES)]

    def softmax_piece(s, m_old):
        parts = tiles(s)
        lane_max = functools.reduce(jnp.maximum, parts)
        mn = jnp.broadcast_to(lane_max.max(axis=-1, keepdims=True), (tq, LANES))
        if m_old is not None:
            mn = jnp.maximum(m_old, mn)
        ps = [jnp.exp2(part - mn) for part in parts]
        return mn, jnp.concatenate([p.astype(BF16) for p in ps], axis=-1), functools.reduce(jnp.add, ps)

    col = lax.broadcasted_iota(jnp.int32, (tq, LANES), 1)
    pad_mask = jnp.where(col < N_META, 0.0, MASK_VALUE).astype(F32)

    def chunk(off, first):
        def scores(h):
            s = _dot_nt(q_ref[:, hsl(h)], k_ref[pl.ds(off, MLA_CK), hsl(h)])
            if first:
                s = jnp.concatenate([s, _dot_nt(q_ref[:, hsl(h)], km_ref[:, hsl(h)]) + pad_mask], axis=-1)
            return s

        def values(p, h):
            o = _dot(p[:, :MLA_CK], v_ref[pl.ds(off, MLA_CK), vsl(h)])
            return o + _dot(p[:, MLA_CK:], vm_ref[:, vsl(h)]) if first else o

        ahead = [scores(h) for h in range(MLA_AHEAD)]
        for h in range(MLA_HEADS):
            s = ahead.pop(0)
            if h + MLA_AHEAD < MLA_HEADS:
                ahead.append(scores(h + MLA_AHEAD))
            if first:
                m_scr[h], p, l_scr[h] = softmax_piece(s, None)
                acc_scr[h] = values(p, h)
            else:
                m = m_scr[h]
                mn, p, lsum = softmax_piece(s, m)
                alpha = jnp.exp2(m - mn)
                m_scr[h] = mn
                l_scr[h] = alpha * l_scr[h] + lsum
                acc_scr[h] = alpha * acc_scr[h] + values(p, h)

    for ci in range(seq // MLA_CK):
        chunk(ci * MLA_CK, ci == 0)
    low = col < MLA_V_DIM
    out = [acc_scr[h] / l_scr[h].sum(axis=-1, keepdims=True) for h in range(MLA_HEADS)]
    for j in range(MLA_HEADS // 2):
        o_ref[:, vsl(2 * j)] = jnp.where(low, out[2 * j], out[2 * j + 1]).astype(BF16)


def _mla(q, k, v, km, vm, batch, seq, meta_off, tq, q_rows_per_batch, q_off):
    nq = q_rows_per_batch // tq
    qspec = lambda c: pl.BlockSpec((tq, c), lambda b, i: (q_off + b * nq + i, 0))
    kvspec = lambda c: pl.BlockSpec((seq, c), lambda b, i: (b, 0))
    mspec = lambda c: pl.BlockSpec((LANES, c), lambda b, i: (meta_off + b, 0))
    ospec = pl.BlockSpec((tq, MLA_DIM), lambda b, i: (b * nq + i, 0))
    return pl.pallas_call(
        functools.partial(_mla_kernel, seq=seq),
        out_shape=jax.ShapeDtypeStruct((batch * q_rows_per_batch, MLA_DIM), BF16),
        grid=(batch, nq),
        in_specs=[qspec(MLA_HEADS * HEAD_SLOT), kvspec(MLA_HEADS * HEAD_SLOT), kvspec(MLA_DIM),
                  mspec(MLA_HEADS * HEAD_SLOT), mspec(MLA_DIM)],
        out_specs=ospec,
        scratch_shapes=[pltpu.VMEM((MLA_HEADS, tq, LANES), F32)] * 3,
        compiler_params=_params(2),
        name="mla",
    )(q, k, v, km, vm)


def _ffn_kernel(h_ref, fpre_ref, fwgu_ref, fwd_ref, fpost_ref, o_ref):
    o_ref[...] = _ffn_half_step(h_ref[...], fpre_ref, fwgu_ref, fwd_ref, fpost_ref)


def _ffn(h, layer, ffn_weights, tm):
    row = pl.BlockSpec((tm, D_MODEL), lambda i: (i, 0))
    return pl.pallas_call(
        _ffn_kernel,
        out_shape=jax.ShapeDtypeStruct(h.shape, F32),
        grid=(h.shape[0] // tm,),
        in_specs=[row] + _ffn_specs(layer),
        out_specs=row,
        compiler_params=_params(1),
        name="ffn",
    )(h, *ffn_weights)


def _mix_ffn_kernel(h_ref, cb_ref, ccu_ref, halo_a_ref, halo_b_ref, halo_c_ref, yna_ref, ymla_ref,
                    convw_ref, cn_ref, nn_ref, mn_ref, wo_ref, post_ref, *rest, meta, tiles_per_seq,
                    prompt_batches):
    o_ref = rest[-1]
    tm = h_ref.shape[0]
    i = pl.program_id(0)
    ccu = ccu_ref[...].astype(F32)
    first_row = lambda ref: ref[...].astype(F32)[0:1, :]
    last_row = lambda ref: ref[...].astype(F32)[HALO_ROWS - 1:HALO_ROWS, :]
    if meta:
        prev_row = jnp.zeros((1, CONV_DIM), F32)
        next_row = jnp.where(i < prompt_batches, first_row(halo_a_ref), first_row(halo_b_ref))
    else:
        t = i % tiles_per_seq
        prev_row = jnp.where(t == 0, last_row(halo_c_ref), last_row(halo_a_ref))
        next_row = jnp.where(t == tiles_per_seq - 1, jnp.zeros((1, CONV_DIM), F32), first_row(halo_b_ref))
    row = lax.broadcasted_iota(jnp.int32, (tm, CONV_DIM), 0)
    dn = jnp.where(row == 0, prev_row, pltpu.roll(ccu, 1, axis=0))
    up = jnp.where(row == tm - 1, next_row, pltpu.roll(ccu, tm - 1, axis=0))
    w = convw_ref[...]
    yc = cb_ref[...].astype(F32) * (w[0:1, :] * dn + w[1:2, :] * ccu + w[2:3, :] * up)
    y = jnp.concatenate([_rms(yc, cn_ref[...]), _rms(yna_ref[...].astype(F32), nn_ref[...]),
                         _rms(ymla_ref[...].astype(F32), mn_ref[...])], axis=-1).astype(BF16)
    x = h_ref[...] + _rms(_dot(y, wo_ref[...]), post_ref[...])
    o_ref[...] = x if meta else _ffn_half_step(x, *rest[:-1])


def _mix_ffn(h, cb, ccu, halo_a, halo_b, halo_c, yna, ymla, layer, w, *, meta, tm, seq=None,
         meta_off=0, prompt_batches=0, seq_a=0, seq_b=0):
    n = h.shape[0]
    row = lambda c: pl.BlockSpec((tm, c), lambda i: (i, 0))
    halo = lambda index_map: pl.BlockSpec((HALO_ROWS, CONV_DIM), index_map)
    if meta:
        tiles_per_seq = 1
        a_spec = halo(lambda i: (jnp.minimum(i, prompt_batches - 1) * (seq_a // HALO_ROWS), 0))
        b_spec = halo(lambda i: (jnp.maximum(i - prompt_batches, 0) * (seq_b // HALO_ROWS), 0))
        c_spec = halo(lambda i: (0, 0))
    else:
        tiles_per_seq = seq // tm
        last = n // HALO_ROWS - 1
        a_spec = halo(lambda i: (jnp.maximum(i * (tm // HALO_ROWS) - 1, 0), 0))
        b_spec = halo(lambda i: (jnp.minimum((i + 1) * (tm // HALO_ROWS), last), 0))
        c_spec = halo(lambda i: (meta_off + i // tiles_per_seq, 0))
    return pl.pallas_call(
        functools.partial(_mix_ffn_kernel, meta=meta, tiles_per_seq=tiles_per_seq, prompt_batches=prompt_batches),
        out_shape=jax.ShapeDtypeStruct(h.shape, F32),
        grid=(n // tm,),
        in_specs=[row(D_MODEL), row(CONV_DIM), row(CONV_DIM), a_spec, b_spec, c_spec,
                  row(NA_DIM), row(MLA_DIM),
                  _const_spec((None, 3, CONV_DIM), (layer, 0, 0)),
                  _const_spec((None, 1, CONV_DIM), (layer, 0, 0)),
                  _const_spec((None, 1, NA_DIM), (layer, 0, 0)),
                  _const_spec((None, 1, MLA_DIM), (layer, 0, 0)),
                  _const_spec((None, D_MODEL, D_MODEL), (layer, 0, 0)),
                  _const_spec((None, 1, D_MODEL), (layer, 0, 0))] + ([] if meta else _ffn_specs(layer)),
        out_specs=row(D_MODEL),
        compiler_params=_params(1),
        name="mix_ffn",
    )(h, cb, ccu, halo_a, halo_b, halo_c, yna, ymla,
      w["conv_w"], w["conv_on"], w["na_on"], w["mla_on"], w["w_o"], w["mix_post"], *(() if meta else w["ffn2"]))


def _na_patterns():
    patterns = []
    per_rows = {}
    for rows in (2048 // GRID_W, 4096 // GRID_W):
        win_h = min(NA_MAX_WIN_H, rows)
        vids, starts = [], []
        for blk in range(rows // NA_QROWS):
            r0 = blk * NA_QROWS
            start = int(np.clip(r0 - win_h // 2, 0, rows - NA_KROWS))
            qr = r0 + np.arange(NA_QROWS)[:, None]
            kr = start + np.arange(NA_KROWS)[None, :]
            rs = np.clip(qr - win_h // 2, 0, rows - win_h)
            inside = (kr >= rs) & (kr < rs + win_h)
            assert (inside.sum(axis=1) == win_h).all()
            rel = np.where(inside, kr - qr + NA_MAX_WIN_H - 1, -1)
            for n, p in enumerate(patterns):
                if np.array_equal(p, rel):
                    vids.append(n)
                    break
            else:
                vids.append(len(patterns))
                patterns.append(rel)
            starts.append(start)
        per_rows[rows] = (np.asarray(vids, np.int32), np.asarray(starts, np.int32))
    return np.stack(patterns), per_rows


_NA_REL_ROWS, _NA_BLOCKS = _na_patterns()


def _na_col_tables():
    qc = np.arange(GRID_W)[:, None]
    kc = np.arange(GRID_W)[None, :]
    cs = np.clip(qc - NA_WIN_W // 2, 0, GRID_W - NA_WIN_W)
    inside = (kc >= cs) & (kc < cs + NA_WIN_W)
    rel = kc - qc + NA_WIN_W - 1
    onehot = (rel[None] == np.arange(2 * NA_WIN_W - 1)[:, None, None]) & inside[None]
    return onehot.astype(np.float32), inside


_NA_COL_ONEHOT, _NA_COL_INSIDE = _na_col_tables()


def _na_bias_tables(rpb, meta_bias):
    blocks = jnp.einsum("lhrd,dqk->lhrqk", rpb.astype(F32), _NA_COL_ONEHOT, precision=lax.Precision.HIGHEST)
    blocks = jnp.where(_NA_COL_INSIDE, blocks * LOG2_E, MASK_VALUE)
    masked = jnp.full(blocks.shape[:2] + (GRID_W, GRID_W), MASK_VALUE, F32)
    meta_cols = jnp.broadcast_to((meta_bias.astype(F32) * LOG2_E)[:, :, None, :],
                                 meta_bias.shape[:2] + (GRID_W, N_META))
    pad_cols = jnp.full(meta_bias.shape[:2] + (GRID_W, NA_KCAT - NA_KB - N_META), MASK_VALUE, F32)
    tables = []
    for rel in _NA_REL_ROWS:
        rows = [jnp.concatenate([masked if r < 0 else blocks[:, :, int(r)] for r in rel_q] + [meta_cols, pad_cols],
                                axis=-1) for rel_q in rel]
        tables.append(jnp.concatenate(rows, axis=-2))
    return jnp.stack(tables, axis=1).reshape(-1, NA_HEADS // 2, 2 * NA_QB, NA_KCAT)


def _rope_table(pos):
    half = MLA_ROPE_DIM // 2
    inv_freq = ROPE_THETA ** (-jnp.arange(half, dtype=F32) / half)
    ang = pos.astype(F32)[:, None] * inv_freq[None, :]
    cos = jnp.concatenate([jnp.cos(ang)] * 2, axis=-1)
    sin = jnp.concatenate([jnp.sin(ang)] * 2, axis=-1)
    return jnp.concatenate([cos, sin, cos, sin], axis=-1)


def _rot_cols(w):
    half = MLA_ROPE_DIM // 2
    return jnp.concatenate([-w[..., half:], w[..., :half]], axis=-1)


def _prep_weights(ffn1_pre, ffn1_gu, ffn1_down, ffn1_post, mix_pre, w_in, conv_w, na_rpb, na_mb, q_norm, w_uq,
                  kv_norm, w_ukv, conv_on, na_on, mla_on, w_o, mix_post, ffn2_pre, ffn2_gu, ffn2_down, ffn2_post):
    vec = lambda g: g[:, None, :]
    kpe = w_in[:, :, 2560:2560 + MLA_ROPE_DIM]
    kpe_rot = _rot_cols(kpe)
    w_in_ext = jnp.concatenate([w_in[:, :, :2560], kpe, kpe_rot, kpe, kpe_rot], axis=-1).astype(BF16)
    per_head = w_uq.reshape(DEPTH, MLA_Q_RANK, MLA_HEADS, MLA_NOPE_DIM + MLA_ROPE_DIM)
    nope, pe = per_head[..., :MLA_NOPE_DIM], per_head[..., MLA_NOPE_DIM:]
    even = jnp.concatenate([nope, pe, _rot_cols(pe)], axis=-1)
    odd = jnp.concatenate([pe, _rot_cols(pe), nope], axis=-1)
    is_even = (jnp.arange(MLA_HEADS) % 2 == 0)[None, None, :, None]
    w_uq_ext = jnp.where(is_even, even, odd).reshape(DEPTH, MLA_Q_RANK, MLA_HEADS * HEAD_SLOT).astype(BF16)
    kv_heads = w_ukv.reshape(DEPTH, MLA_KV_RANK, MLA_HEADS, MLA_NOPE_DIM + MLA_V_DIM)
    w_uk = kv_heads[..., :MLA_NOPE_DIM].reshape(DEPTH, MLA_KV_RANK, MLA_HEADS * MLA_NOPE_DIM).astype(BF16)
    w_uv = kv_heads[..., MLA_NOPE_DIM:].reshape(DEPTH, MLA_KV_RANK, MLA_DIM).astype(BF16)
    bias = _na_bias_tables(na_rpb, na_mb)
    return dict(
        ffn1=(vec(ffn1_pre), ffn1_gu.astype(BF16), ffn1_down.astype(BF16), vec(ffn1_post)),
        ffn2=(vec(ffn2_pre), ffn2_gu.astype(BF16), ffn2_down.astype(BF16), vec(ffn2_post)),
        mix_pre=vec(mix_pre), w_in=w_in_ext, q_norm=vec(q_norm), w_uq=w_uq_ext, kv_norm=vec(kv_norm),
        w_uk=w_uk, w_uv=w_uv, conv_w=conv_w, conv_on=vec(conv_on), na_on=vec(na_on), mla_on=vec(mla_on),
        w_o=w_o.astype(BF16), mix_post=vec(mix_post), na_bias=bias, na_mb=na_mb)


def _pad_meta_rows(x):
    c = x.shape[-1]
    x = x.reshape(-1, N_META, c)
    return jnp.pad(x, ((0, 0), (0, LANES - N_META), (0, 0))).reshape(-1, c)


def kernel(x_prompt, x_sample, meta_tokens, ffn1_pre_norm, ffn1_w_gu, ffn1_w_down, ffn1_post_norm, mix_pre_norm,
           w_in, conv_w, na_rpb, na_meta_bias, mla_q_norm, mla_w_uq, mla_kv_norm, mla_w_ukv, conv_out_norm,
           na_out_norm, mla_out_norm, w_o, mix_post_norm, ffn2_pre_norm, ffn2_w_gu, ffn2_w_down, ffn2_post_norm):
    w = _prep_weights(ffn1_pre_norm, ffn1_w_gu, ffn1_w_down, ffn1_post_norm, mix_pre_norm, w_in, conv_w, na_rpb,
                      na_meta_bias, mla_q_norm, mla_w_uq, mla_kv_norm, mla_w_ukv, conv_out_norm, na_out_norm,
                      mla_out_norm, w_o, mix_post_norm, ffn2_pre_norm, ffn2_w_gu, ffn2_w_down, ffn2_post_norm)
    groups = []
    for x in (x_prompt, x_sample):
        b, t, _ = x.shape
        groups.append(dict(batch=b, seq=t, te=_rope_table(N_META + jnp.arange(t))))
    nb = [g["batch"] for g in groups]
    meta_off = [0, nb[0]]
    n_meta_rows = sum(nb) * N_META
    te_meta = jnp.tile(_rope_table(jnp.arange(N_META)), (sum(nb), 1))
    hs = [x_prompt.reshape(-1, D_MODEL), x_sample.reshape(-1, D_MODEL),
          jnp.tile(meta_tokens.astype(F32), (sum(nb), 1))]

    for layer in range(DEPTH):
        proj = [_ffn_inproj(hs[g], layer, w, groups[g]["te"], groups[g]["seq"] // ROW_TILE, ROW_TILE)
                for g in range(2)]
        proj.append(_ffn_inproj(hs[2], layer, w, te_meta, 1, n_meta_rows))
        hs = [p[0] for p in proj]
        cb_m, ccu_m, nq_m, nk_m, nv_m, q_m, k_m, v_m = proj[2][1:]
        k_mp, v_mp = _pad_meta_rows(k_m), _pad_meta_rows(v_m)
        meta_live = layer < DEPTH - 1
        new_hs, ymla_m = [], []
        for g in range(2):
            cb, ccu, nq, nk, nv, q, k, v = proj[g][1:]
            b, t = groups[g]["batch"], groups[g]["seq"]
            vid, st = _NA_BLOCKS[t // GRID_W]
            yna = _na(nq, nk, nv, nk_m, nv_m, w["na_bias"], layer,
                      jnp.asarray(vid), jnp.asarray(st), b, t, meta_off[g])
            ymla = _mla(q, k, v, k_mp, v_mp, b, t, meta_off[g], MLA_TQ, t, 0)
            if meta_live:
                ymla_m.append(_mla(q_m, k, v, k_mp, v_mp, b, t, meta_off[g], N_META, N_META, meta_off[g]))
            new_hs.append(_mix_ffn(hs[g], cb, ccu, ccu, ccu, ccu_m, yna, ymla, layer, w, meta=False, tm=ROW_TILE,
                                   seq=t, meta_off=meta_off[g]))
        if meta_live:
            yna_m = _na_meta(nq_m, nk_m, nv_m, w["na_mb"], layer)
            mixed_m = _mix_ffn(hs[2], cb_m, ccu_m, proj[0][2], proj[1][2], ccu_m, yna_m,
                               jnp.concatenate(ymla_m, axis=0), layer, w, meta=True, tm=N_META,
                               prompt_batches=nb[0], seq_a=groups[0]["seq"], seq_b=groups[1]["seq"])
            new_hs.append(_ffn(mixed_m, layer, w["ffn2"], n_meta_rows))
        hs = new_hs
    return (hs[0].reshape(x_prompt.shape), hs[1].reshape(x_sample.shape))
```
